```python
import jax, jax.numpy as jnp
from jax import lax
import numpy as np


D_MODEL = 1024
BATCH = 8
SEQ = 2048
DEPTH = 4

HEAD_DIM = 64
ROPE_DIM = HEAD_DIM // 4
ROPE_THETA = 500000.0
NORM_EPS = 1e-6
NEG = -1e30
BAND_BLOCK = 128
MAX_POS_OFFSET = 4096

NSA_HEADS = D_MODEL // HEAD_DIM
NSA_KV_HEADS = 4
NSA_GROUP = NSA_HEADS // NSA_KV_HEADS
CMP_BLOCK = 32
CMP_STRIDE = 16
CMP_HIDDEN = 4 * HEAD_DIM
SEL_BLOCK = 64
SEL_TOPN = 8
FORCE_SCORE = 1e4
WIN = 512
NSA_Q_CHUNK = 64
NSA_IN = NSA_HEADS * HEAD_DIM + 6 * NSA_KV_HEADS * HEAD_DIM + 3 * NSA_HEADS

DIL_PATTERNS = ((128, 1), (512, 4), (2048, 16))
N_DIL = len(DIL_PATTERNS)
DIL_HEADS = D_MODEL // HEAD_DIM
DIL_IN = N_DIL * 3 * DIL_HEADS * HEAD_DIM

D_FF = ((8 * D_MODEL + 3 * 256 - 1) // (3 * 256)) * 256

N_NSA_LAYERS = (DEPTH + 1) // 2
N_DIL_LAYERS = DEPTH // 2

kernel_name = 'hybrid_nsa_dilated_swiglu_trunk'


def rmsnorm(x, g):
    xf = x.astype(jnp.float32)
    y = xf * lax.rsqrt(jnp.mean(xf * xf, axis=-1, keepdims=True) + NORM_EPS)
    return (y * g.astype(jnp.float32)).astype(x.dtype)


def rope_tables(positions):
    inv = ROPE_THETA ** (-jnp.arange(0, ROPE_DIM, 2, dtype=jnp.float32) / ROPE_DIM)
    ang = positions.astype(jnp.float32)[..., None] * inv
    return jnp.cos(ang)[:, :, None, :], jnp.sin(ang)[:, :, None, :]


def partial_rope(x, cos, sin):
    half = ROPE_DIM // 2
    x1 = x[..., :half].astype(jnp.float32)
    x2 = x[..., half:ROPE_DIM].astype(jnp.float32)
    rot = jnp.concatenate([x1 * cos - x2 * sin, x2 * cos + x1 * sin], axis=-1).astype(x.dtype)
    return jnp.concatenate([rot, x[..., ROPE_DIM:]], axis=-1)


def banded_attention(q, k, v, max_dist):
    n, L, hkv, g, dh = q.shape
    nb = -(-L // BAND_BLOCK)
    Lp = nb * BAND_BLOCK
    kw = BAND_BLOCK + max_dist
    qp = jnp.pad(q, ((0, 0), (0, Lp - L), (0, 0), (0, 0), (0, 0)))
    kp = jnp.pad(k, ((0, 0), (max_dist, Lp - L), (0, 0), (0, 0)))
    vp = jnp.pad(v, ((0, 0), (max_dist, Lp - L), (0, 0), (0, 0)))
    qb = qp.reshape(n, nb, BAND_BLOCK, hkv, g, dh).swapaxes(0, 1)
    rel = jnp.arange(BAND_BLOCK)[:, None] + max_dist - jnp.arange(kw)[None, :]
    band = (rel >= 0) & (rel <= max_dist)
    scale = dh ** -0.5

    def block(args):
        c, qc = args
        start = c * BAND_BLOCK
        kc = lax.dynamic_slice_in_dim(kp, start, kw, axis=1)
        vc = lax.dynamic_slice_in_dim(vp, start, kw, axis=1)
        kpos = start - max_dist + jnp.arange(kw)
        mask = band & (kpos >= 0)[None, :]
        s = jnp.einsum('nqhgd,nkhd->nhgqk', qc, kc).astype(jnp.float32) * scale
        s = jnp.where(mask, s, NEG)
        m = jnp.max(s, axis=-1, keepdims=True)
        p = jnp.exp(s - m)
        l = jnp.sum(p, axis=-1, keepdims=True)
        o = jnp.einsum('nhgqk,nkhd->nqhgd', (p / l).astype(vc.dtype), vc)
        lse = (m + jnp.log(l))[..., 0].transpose(0, 3, 1, 2)
        return o, lse

    o, lse = lax.map(block, (jnp.arange(nb), qb))
    o = o.swapaxes(0, 1).reshape(n, Lp, hkv, g, dh)[:, :L]
    lse = lse.swapaxes(0, 1).reshape(n, Lp, hkv, g)[:, :L]
    return o, lse


def nsa_mixer(a, cos, sin, w_in, cmp_pos, cmp_wk1, cmp_wk2, cmp_wv1, cmp_wv2, w_o):
    B, S, _ = a.shape
    H, HK, G, DH = NSA_HEADS, NSA_KV_HEADS, NSA_GROUP, HEAD_DIM
    dq, dkv = H * DH, HK * DH
    scale = DH ** -0.5
    proj = a @ w_in
    q = proj[..., :dq].reshape(B, S, H, DH)
    kv = proj[..., dq:dq + 6 * dkv].reshape(B, S, 6, HK, DH)
    k_cmp, v_cmp, k_sel, v_sel, k_win, v_win = [kv[:, :, i] for i in range(6)]
    gates = jax.nn.sigmoid(proj[..., dq + 6 * dkv:].astype(jnp.float32)).reshape(B, S, 3, HK, G)
    q_grp = q.reshape(B, S, HK, G, DH)
    q_rot = partial_rope(q, cos, sin).reshape(B, S, HK, G, DH)
    k_sel = partial_rope(k_sel, cos, sin)
    k_win = partial_rope(k_win, cos, sin)
    pos = jnp.arange(S)

    n_cmp = (S - CMP_BLOCK) // CMP_STRIDE + 1
    blk = jnp.arange(n_cmp)[:, None] * CMP_STRIDE + jnp.arange(CMP_BLOCK)[None, :]

    def compress(t, w1, w2):
        tb = t[:, blk] + cmp_pos[None, None, :, None, :]
        tb = tb.transpose(0, 1, 3, 2, 4).reshape(B, n_cmp, HK, CMP_BLOCK * DH)
        return jax.nn.silu(tb @ w1) @ w2

    kc = compress(k_cmp, cmp_wk1, cmp_wk2)
    vc = compress(v_cmp, cmp_wv1, cmp_wv2)
    c_start = jnp.arange(n_cmp) * CMP_STRIDE
    c_valid = (c_start + CMP_BLOCK - 1)[None, :] <= pos[:, None]
    s_cmp = jnp.einsum('bthgd,bchd->bhgtc', q_grp, kc).astype(jnp.float32) * scale
    p_cmp = jax.nn.softmax(jnp.where(c_valid, s_cmp, NEG), axis=-1) * c_valid
    o_cmp = jnp.einsum('bhgtc,bchd->bthgd', p_cmp.astype(vc.dtype), vc)

    n_sel = S // SEL_BLOCK
    top_n = min(SEL_TOPN, n_sel)
    j_start = jnp.arange(n_sel) * SEL_BLOCK
    overlap = ((c_start[:, None] < j_start[None, :] + SEL_BLOCK)
               & (c_start[:, None] + CMP_BLOCK > j_start[None, :])).astype(jnp.float32)
    imp = jnp.einsum('bhgtc,cj->bhtj', p_cmp, overlap)
    jb = jnp.arange(n_sel)[None, :]
    tb = (pos // SEL_BLOCK)[:, None]
    forced = (jb == 0) | (jb == tb) | (jb == tb - 1)
    score = jnp.where(forced, FORCE_SCORE, jnp.where(jb <= tb, imp, -1.0))
    _, sel_idx = lax.top_k(score, top_n)

    n_ch = S // NSA_Q_CHUNK
    n_key = top_n * SEL_BLOCK
    kb = k_sel.reshape(B, n_sel, SEL_BLOCK, HK, DH).transpose(0, 3, 1, 2, 4)
    vb = v_sel.reshape(B, n_sel, SEL_BLOCK, HK, DH).transpose(0, 3, 1, 2, 4)
    q_ch = q_rot.reshape(B, n_ch, NSA_Q_CHUNK, HK, G, DH).swapaxes(0, 1)
    i_ch = sel_idx.reshape(B, HK, n_ch, NSA_Q_CHUNK, top_n).transpose(2, 0, 1, 3, 4)
    gather = jax.vmap(jax.vmap(lambda blocks, ids: blocks[ids]))

    def sel_chunk(args):
        c, qc, ic = args
        kg = gather(kb, ic).reshape(B, HK, NSA_Q_CHUNK, n_key, DH)
        vg = gather(vb, ic).reshape(B, HK, NSA_Q_CHUNK, n_key, DH)
        kpos = (ic[..., None] * SEL_BLOCK + jnp.arange(SEL_BLOCK)).reshape(B, HK, NSA_Q_CHUNK, n_key)
        qpos = c * NSA_Q_CHUNK + jnp.arange(NSA_Q_CHUNK)
        mask = (kpos <= qpos[:, None])[:, :, None]
        s = jnp.einsum('bqhgd,bhqkd->bhgqk', qc, kg).astype(jnp.float32) * scale
        p = jax.nn.softmax(jnp.where(mask, s, NEG), axis=-1)
        return jnp.einsum('bhgqk,bhqkd->bqhgd', p.astype(vg.dtype), vg)

    o_sel = lax.map(sel_chunk, (jnp.arange(n_ch), q_ch, i_ch))
    o_sel = o_sel.swapaxes(0, 1).reshape(B, S, HK, G, DH)

    o_win, _ = banded_attention(q_rot, k_win, v_win, WIN - 1)

    branches = jnp.stack([o_cmp, o_sel, o_win], axis=2)
    o = jnp.sum(gates[..., None].astype(branches.dtype) * branches, axis=2)
    return o.reshape(B, S, H * DH) @ w_o


def to_residue(t, dil):
    B, S, H, DH = t.shape
    return t.reshape(B, S // dil, dil, H, DH).swapaxes(1, 2).reshape(B * dil, S // dil, H, DH)


def from_residue(t, B, dil):
    rest = t.shape[2:]
    L = t.shape[1]
    return t.reshape((B, dil, L) + rest).swapaxes(1, 2).reshape((B, L * dil) + rest)


def dilated_mixer(a, cos, sin, w_in, w_o):
    B, S, _ = a.shape
    H, DH = DIL_HEADS, HEAD_DIM
    proj = (a @ w_in).reshape(B, S, N_DIL, 3, H, DH)
    outs, lses = [], []
    for gi, (window, dil) in enumerate(DIL_PATTERNS):
        q = partial_rope(proj[:, :, gi, 0], cos, sin)
        k = partial_rope(proj[:, :, gi, 1], cos, sin)
        v = proj[:, :, gi, 2]
        o, lse = banded_attention(to_residue(q, dil)[:, :, :, None], to_residue(k, dil),
                                  to_residue(v, dil), window // dil)
        outs.append(from_residue(o[:, :, :, 0], B, dil))
        lses.append(from_residue(lse[:, :, :, 0], B, dil))
    alpha = jax.nn.softmax(jnp.stack(lses, axis=-1), axis=-1)
    o = jnp.einsum('bshn,bshnd->bshd', alpha.astype(a.dtype), jnp.stack(outs, axis=3))
    return o.reshape(B, S, H * DH) @ w_o


def swiglu(h, w_gu, w_down):
    gu = h @ w_gu
    g, u = gu[..., :D_FF], gu[..., D_FF:]
    return (jax.nn.silu(g) * u) @ w_down


def setup_inputs(seed: int = 0) -> dict:
    key = jax.random.key(seed)
    ks = jax.random.split(key, 20)
    f32 = jnp.float32

    def nrm(k, shape, fan_in):
        return jax.random.normal(k, shape, f32) * (fan_in ** -0.5)

    def gain(k, shape):
        return 1.0 + 0.02 * jax.random.normal(k, shape, f32)

    nA, nB = N_NSA_LAYERS, N_DIL_LAYERS
    x = jax.random.normal(ks[0], (BATCH, SEQ, D_MODEL), f32)
    positions = (jax.random.randint(ks[1], (BATCH, 1), 0, MAX_POS_OFFSET, jnp.int32)
                 + jnp.arange(SEQ, dtype=jnp.int32)[None, :])
    return {
        'x': x,
        'positions': positions,
        'norm_mix_pre': gain(ks[2], (DEPTH, D_MODEL)),
        'norm_mix_post': gain(ks[3], (DEPTH, D_MODEL)),
        'norm_ffn_pre': gain(ks[4], (DEPTH, D_MODEL)),
        'norm_ffn_post': gain(ks[5], (DEPTH, D_MODEL)),
        'ffn_w_gu': nrm(ks[6], (DEPTH, D_MODEL, 2 * D_FF), D_MODEL),
        'ffn_w_down': nrm(ks[7], (DEPTH, D_FF, D_MODEL), D_FF),
        'nsa_w_in': nrm(ks[8], (nA, D_MODEL, NSA_IN), D_MODEL),
        'nsa_cmp_pos': 0.1 * jax.random.normal(ks[9], (nA, CMP_BLOCK, HEAD_DIM), f32),
        'nsa_cmp_wk1': nrm(ks[10], (nA, CMP_BLOCK * HEAD_DIM, CMP_HIDDEN), CMP_BLOCK * HEAD_DIM),
        'nsa_cmp_wk2': nrm(ks[11], (nA, CMP_HIDDEN, HEAD_DIM), CMP_HIDDEN),
        'nsa_cmp_wv1': nrm(ks[12], (nA, CMP_BLOCK * HEAD_DIM, CMP_HIDDEN), CMP_BLOCK * HEAD_DIM),
        'nsa_cmp_wv2': nrm(ks[13], (nA, CMP_HIDDEN, HEAD_DIM), CMP_HIDDEN),
        'nsa_w_o': nrm(ks[14], (nA, NSA_HEADS * HEAD_DIM, D_MODEL), NSA_HEADS * HEAD_DIM),
        'dil_w_in': nrm(ks[15], (nB, D_MODEL, DIL_IN), D_MODEL),
        'dil_w_o': nrm(ks[16], (nB, DIL_HEADS * HEAD_DIM, D_MODEL), DIL_HEADS * HEAD_DIM),
    }


def reference(x, positions, norm_mix_pre, norm_mix_post, norm_ffn_pre, norm_ffn_post,
              ffn_w_gu, ffn_w_down, nsa_w_in, nsa_cmp_pos, nsa_cmp_wk1, nsa_cmp_wk2,
              nsa_cmp_wv1, nsa_cmp_wv2, nsa_w_o, dil_w_in, dil_w_o):
    cos, sin = rope_tables(positions)
    h = x
    for i in range(DEPTH):
        j = i // 2
        a = rmsnorm(h, norm_mix_pre[i])
        if i % 2 == 0:
            m = nsa_mixer(a, cos, sin, nsa_w_in[j], nsa_cmp_pos[j], nsa_cmp_wk1[j], nsa_cmp_wk2[j],
                          nsa_cmp_wv1[j], nsa_cmp_wv2[j], nsa_w_o[j])
        else:
            m = dilated_mixer(a, cos, sin, dil_w_in[j], dil_w_o[j])
        h = h + rmsnorm(m, norm_mix_post[i])
        f = swiglu(rmsnorm(h, norm_ffn_pre[i]), ffn_w_gu[i], ffn_w_down[i])
        h = h + rmsnorm(f, norm_ffn_post[i])
    return h
```

```python
import functools

import jax
import jax.numpy as jnp
from jax import lax
from jax.experimental import pallas as pl
from jax.experimental.pallas import tpu as pltpu

F32 = jnp.float32
BF16 = jnp.bfloat16

D_MODEL = 1024
HEAD_DIM = 64
ROPE_DIM = HEAD_DIM // 4
ROPE_HALF = ROPE_DIM // 2
ROPE_THETA = 500000.0
NORM_EPS = 1e-6
NEG = -1e30
SCALE = HEAD_DIM ** -0.5

NSA_HEADS = 16
NSA_KV_HEADS = 4
NSA_GROUP = NSA_HEADS // NSA_KV_HEADS
CMP_BLOCK = 32
CMP_STRIDE = 16
CMP_HIDDEN = 4 * HEAD_DIM
SEL_BLOCK = 64
SEL_SHIFT = 6
SEL_TOPN = 8
FORCE_SCORE = 1e4
WIN = 512
NSA_Q = NSA_HEADS * HEAD_DIM
NSA_KV = NSA_KV_HEADS * 2 * HEAD_DIM
NSA_GATES = 3 * NSA_HEADS
NSA_W_COLS = NSA_Q + 3 * NSA_KV + 128

DIL_PATTERNS = ((128, 1), (512, 4), (2048, 16))
DIL_HEADS = 16
DIL_BAND_BLOCK = 128

D_FF = 2816

LANES = 128
VMEM_LIMIT = 56 * 1024 * 1024


def _cparams(sem):
    return pltpu.CompilerParams(dimension_semantics=sem, vmem_limit_bytes=VMEM_LIMIT)


def _rms(x, g):
    ms = jnp.mean(x * x, axis=-1, keepdims=True)
    return x * lax.rsqrt(ms + NORM_EPS) * g


def _sigmoid(x):
    return 1.0 / (1.0 + jnp.exp(-x))


def _rope128(x, c, s1, s2):
    return x * c + pltpu.roll(x, ROPE_HALF, 1) * s1 + pltpu.roll(x, LANES - ROPE_HALF, 1) * s2


def _rope_wide(x, c, s1, s2):
    n = x.shape[1] // LANES
    return jnp.concatenate(
        [_rope128(x[:, i * LANES:(i + 1) * LANES], c, s1, s2) for i in range(n)], axis=1)


def _dot(a, b):
    return jnp.dot(a, b, preferred_element_type=F32)


def _dot_t(a, b):
    return lax.dot_general(a, b, (((1,), (1,)), ((), ())), preferred_element_type=F32)


def _rope_tables(positions):
    inv = ROPE_THETA ** (-jnp.arange(0, ROPE_DIM, 2, dtype=F32) / ROPE_DIM)
    ang = positions.astype(F32)[..., None] * inv
    cos, sin = jnp.cos(ang), jnp.sin(ang)
    one = jnp.ones(ang.shape[:-1] + (HEAD_DIM - ROPE_DIM,), F32)
    zero = jnp.zeros_like(one)
    z8 = jnp.zeros_like(sin)
    c64 = jnp.concatenate([cos, cos, one], axis=-1)
    s1 = jnp.concatenate([z8, sin, zero], axis=-1)
    s2 = jnp.concatenate([-sin, z8, zero], axis=-1)
    tile = lambda t: jnp.concatenate([t, t], axis=-1)
    return tile(c64), tile(s1), tile(s2)


def _nsa_proj_kernel(h_ref, g_ref, w_ref, q_ref, kvc_ref, kvs_ref, kvw_ref, gate_ref):
    a = _rms(h_ref[0], g_ref[...]).astype(BF16)
    r = _dot(a, w_ref[...])
    q_ref[0] = r[:, :NSA_Q].astype(BF16)
    for i, ref in enumerate((kvc_ref, kvs_ref, kvw_ref)):
        base = NSA_Q + i * NSA_KV
        for hk in range(NSA_KV_HEADS):
            ref[0, hk] = r[:, base + hk * LANES: base + (hk + 1) * LANES].astype(BF16)
    gates = _sigmoid(r[:, NSA_Q + 3 * NSA_KV:])
    for hk in range(NSA_KV_HEADS):
        shift = (LANES - NSA_GROUP * hk) % LANES
        gate_ref[0, hk] = pltpu.roll(gates, shift, 1) if shift else gates


def _nsa_proj(h, g_pre, w, tm=512):
    B, S, D = h.shape
    kv_shape = jax.ShapeDtypeStruct((B, NSA_KV_HEADS, S, LANES), BF16)
    kv_spec = pl.BlockSpec((1, NSA_KV_HEADS, tm, LANES), lambda b, i: (b, 0, i, 0))
    return pl.pallas_call(
        _nsa_proj_kernel,
        grid=(B, S // tm),
        in_specs=[
            pl.BlockSpec((1, tm, D), lambda b, i: (b, i, 0)),
            pl.BlockSpec((1, D), lambda b, i: (0, 0)),
            pl.BlockSpec((D, NSA_W_COLS), lambda b, i: (0, 0)),
        ],
        out_specs=[
            pl.BlockSpec((1, tm, NSA_Q), lambda b, i: (b, i, 0)),
            kv_spec, kv_spec, kv_spec,
            pl.BlockSpec((1, NSA_KV_HEADS, tm, LANES), lambda b, i: (b, 0, i, 0)),
        ],
        out_shape=[
            jax.ShapeDtypeStruct((B, S, NSA_Q), BF16),
            kv_shape, kv_shape, kv_shape,
            jax.ShapeDtypeStruct((B, NSA_KV_HEADS, S, LANES), F32),
        ],
        compiler_params=_cparams(("parallel", "parallel")),
        name="nsa_proj",
    )(h, g_pre, w)


def _nsa_in_weight(w_in):
    dq, dkv = NSA_Q, NSA_KV_HEADS * HEAD_DIM
    kv = w_in[:, dq:dq + 6 * dkv].reshape(D_MODEL, 3, 2, NSA_KV_HEADS, HEAD_DIM)
    kv = kv.transpose(0, 1, 3, 2, 4).reshape(D_MODEL, 3 * NSA_KV)
    gates = jnp.pad(w_in[:, dq + 6 * dkv:], ((0, 0), (0, LANES - NSA_GATES)))
    return jnp.concatenate([w_in[:, :dq], kv, gates], axis=1).astype(BF16)


N_CHUNK_TOK = CMP_STRIDE
CHUNK_COLS = N_CHUNK_TOK * 2 * HEAD_DIM


def _compress_kernel(x_ref, pa_ref, pb_ref, wa_ref, wb_ref, w2_ref, o_ref):
    x = x_ref[0, 0].astype(F32)
    ha = _dot((x + pa_ref[...]).astype(BF16), wa_ref[...])
    hb = _dot((x + pb_ref[...]).astype(BF16), wb_ref[...])
    n = x.shape[0]
    hid = ha + pltpu.roll(hb, n - 1, 0)
    act = hid * _sigmoid(hid)
    o_ref[0, 0] = _dot(act.astype(BF16), w2_ref[...])


def _compress(kvc, pa, pb, wa, wb, w2):
    B, HK, S, _ = kvc.shape
    nchunk = S // N_CHUNK_TOK
    x = kvc.reshape(B, HK, nchunk, CHUNK_COLS)
    full = lambda shape: pl.BlockSpec(shape, lambda b, h: (0,) * len(shape))
    return pl.pallas_call(
        _compress_kernel,
        grid=(B, HK),
        in_specs=[
            pl.BlockSpec((1, 1, nchunk, CHUNK_COLS), lambda b, h: (b, h, 0, 0)),
            full((1, CHUNK_COLS)), full((1, CHUNK_COLS)),
            full((CHUNK_COLS, 2 * CMP_HIDDEN)), full((CHUNK_COLS, 2 * CMP_HIDDEN)),
            full((2 * CMP_HIDDEN, LANES)),
        ],
        out_specs=pl.BlockSpec((1, 1, nchunk, LANES), lambda b, h: (b, h, 0, 0)),
        out_shape=jax.ShapeDtypeStruct((B, HK, nchunk, LANES), F32),
        compiler_params=_cparams(("parallel", "parallel")),
        name="nsa_compress",
    )(x, pa, pb, wa, wb, w2)


def _compress_weights(cmp_pos, wk1, wk2, wv1, wv2):
    half = N_CHUNK_TOK * HEAD_DIM

    def first_layer(lo):
        wk = wk1[lo:lo + half].reshape(N_CHUNK_TOK, HEAD_DIM, CMP_HIDDEN)
        wv = wv1[lo:lo + half].reshape(N_CHUNK_TOK, HEAD_DIM, CMP_HIDDEN)
        z = jnp.zeros_like(wk)
        w = jnp.stack([jnp.concatenate([wk, z], -1), jnp.concatenate([z, wv], -1)], axis=1)
        return w.reshape(CHUNK_COLS, 2 * CMP_HIDDEN).astype(BF16)

    def pos(lo):
        p = cmp_pos[lo:lo + N_CHUNK_TOK]
        return jnp.stack([p, p], axis=1).reshape(1, CHUNK_COLS)

    z2 = jnp.zeros_like(wk2)
    w2 = jnp.concatenate([jnp.concatenate([wk2, z2], 1), jnp.concatenate([z2, wv2], 1)], 0)
    return pos(0), pos(N_CHUNK_TOK), first_layer(0), first_layer(half), w2.astype(BF16)


def _flash(qs, kv_ref, j_lo, j_hi, mask_fn, tq, tkv):
    G = NSA_GROUP

    def body(j, carry):
        m, l, acc = carry
        kv = kv_ref[pl.ds(pl.multiple_of(j * tkv, tkv), tkv), :]
        s = _dot_t(qs, kv).reshape(G, tq, tkv)
        s = jnp.where(mask_fn(j)[None], s, NEG)
        m_new = jnp.maximum(m, jnp.max(s, axis=-1, keepdims=True))
        p = jnp.exp(s - m_new)
        a = jnp.exp(m - m_new)
        l = a * l + jnp.sum(p, axis=-1, keepdims=True)
        pv = _dot(p.reshape(G * tq, tkv).astype(BF16), kv)
        acc = a.reshape(G * tq, 1) * acc + pv
        return m_new, l, acc

    init = (jnp.full((G, tq, 1), NEG, F32), jnp.zeros((G, tq, 1), F32),
            jnp.zeros((G * tq, LANES), F32))
    _, l, acc = lax.fori_loop(j_lo, j_hi, body, init)
    return acc / l.reshape(G * tq, 1)


def _nsa_attn_kernel(q_ref, qc_ref, qs1_ref, qs2_ref, kc_ref, ks1_ref, ks2_ref,
                     kvc_ref, kvs_ref, kvw_ref, gate_ref, o_ref, ks_scr, kw_scr,
                     *, tq, tkv):
    G = NSA_GROUP
    qi = pl.program_id(2)
    S = kvs_ref.shape[2]
    n_sel = S // SEL_BLOCK

    @pl.when(qi == 0)
    def _():
        is_key = lax.broadcasted_iota(jnp.int32, (S, LANES), 1) < HEAD_DIM
        c = jnp.where(is_key, kc_ref[0], 1.0)
        s1 = jnp.where(is_key, ks1_ref[0], 0.0)
        s2 = jnp.where(is_key, ks2_ref[0], 0.0)
        ks_scr[...] = _rope128(kvs_ref[0, 0].astype(F32), c, s1, s2).astype(BF16)
        kw_scr[...] = _rope128(kvw_ref[0, 0].astype(F32), c, s1, s2).astype(BF16)

    lane = lax.broadcasted_iota(jnp.int32, (tq, LANES), 1)
    row = lax.broadcasted_iota(jnp.int32, (tq, LANES), 0)
    lo = lane < HEAD_DIM
    t_pos = qi * tq + row

    def stack(x):
        parts = []
        for c2 in range(G // 2):
            xc = x[:, c2 * LANES:(c2 + 1) * LANES]
            parts.append(jnp.where(lo, xc, 0.0))
            parts.append(jnp.where(lo, pltpu.roll(xc, HEAD_DIM, 1), 0.0))
        return (jnp.concatenate(parts, axis=0) * SCALE).astype(BF16)

    q = q_ref[0].astype(F32)
    qs_plain = stack(q)
    qs_rot = stack(_rope_wide(q, qc_ref[0], qs1_ref[0], qs2_ref[0]))

    kvc = kvc_ref[0, 0].astype(BF16)
    c_valid = (CMP_STRIDE * lane + CMP_BLOCK - 1) <= t_pos
    s = _dot_t(qs_plain, kvc).reshape(G, tq, LANES)
    s = jnp.where(c_valid[None], s, NEG)
    e = jnp.exp(s - jnp.max(s, axis=-1, keepdims=True))
    p_cmp = e / jnp.sum(e, axis=-1, keepdims=True) * jnp.where(c_valid, 1.0, 0.0)[None]
    o_cmp = _dot(p_cmp.reshape(G * tq, LANES).astype(BF16), kvc)

    p_sum = p_cmp[0]
    for g in range(1, G):
        p_sum = p_sum + p_cmp[g]
    cb = lax.broadcasted_iota(jnp.int32, (LANES, LANES), 0) * CMP_STRIDE
    jb0 = lax.broadcasted_iota(jnp.int32, (LANES, LANES), 1) * SEL_BLOCK
    overlap = jnp.where((cb < jb0 + SEL_BLOCK) & (cb + CMP_BLOCK > jb0), 1.0, 0.0).astype(BF16)
    p_hi = p_sum.astype(BF16)
    p_lo = (p_sum - p_hi.astype(F32)).astype(BF16)
    imp = _dot(p_hi, overlap) + _dot(p_lo, overlap)
    tb = jnp.right_shift(t_pos, SEL_SHIFT)
    forced = (lane == 0) | (lane == tb) | (lane == tb - 1)
    score = jnp.where(forced, FORCE_SCORE, jnp.where(lane <= tb, imp, -1.0))
    score = jnp.where(lane < n_sel, score, -2.0)
    rank = jnp.zeros((tq, LANES), F32)
    for j2 in range(n_sel):
        col = score[:, j2:j2 + 1]
        beats = (col > score) | ((col == score) & (lane > j2))
        rank = rank + jnp.where(beats, 1.0, 0.0)
    chosen = jnp.where(rank < min(SEL_TOPN, n_sel), 1.0, 0.0).astype(BF16)

    col_k = lax.broadcasted_iota(jnp.int32, (tq, tkv), 1)
    row_q = qi * tq + lax.broadcasted_iota(jnp.int32, (tq, tkv), 0)
    e_row = lax.broadcasted_iota(jnp.int32, (LANES, tkv), 0)
    e_col = lax.broadcasted_iota(jnp.int32, (LANES, tkv), 1)

    def sel_mask(j):
        expand = jnp.where(jnp.right_shift(j * tkv + e_col, SEL_SHIFT) == e_row, 1.0, 0.0).astype(BF16)
        member = _dot(chosen, expand)
        return (member > 0.5) & (j * tkv + col_k <= row_q)

    def win_mask(j):
        d = row_q - (j * tkv + col_k)
        return (d >= 0) & (d <= WIN - 1)

    hi_chunk = (qi * tq + tq + tkv - 1) // tkv
    o_sel = _flash(qs_rot, ks_scr, 0, hi_chunk, sel_mask, tq, tkv)
    lo_chunk = jnp.maximum(qi * tq - (WIN - 1), 0) // tkv
    o_win = _flash(qs_rot, kw_scr, lo_chunk, hi_chunk, win_mask, tq, tkv)

    gate = gate_ref[0, 0]
    heads = []
    for g in range(G):
        rows = slice(g * tq, (g + 1) * tq)
        heads.append(gate[:, g:g + 1] * o_cmp[rows]
                     + gate[:, NSA_HEADS + g:NSA_HEADS + g + 1] * o_sel[rows]
                     + gate[:, 2 * NSA_HEADS + g:2 * NSA_HEADS + g + 1] * o_win[rows])
    out = [jnp.where(lo, pltpu.roll(heads[2 * c2], HEAD_DIM, 1), heads[2 * c2 + 1])
           for c2 in range(G // 2)]
    o_ref[0] = jnp.concatenate(out, axis=1).astype(o_ref.dtype)


def _nsa_attention(q, tabs, kvcmp, kvs, kvw, gates, tq=256, tkv=256):
    B, S, _ = q.shape
    HK = NSA_KV_HEADS
    gw = NSA_GROUP * HEAD_DIM
    tab_q = pl.BlockSpec((1, tq, LANES), lambda b, h, i: (b, i, 0))
    tab_k = pl.BlockSpec((1, S, LANES), lambda b, h, i: (b, 0, 0))
    kv_spec = pl.BlockSpec((1, 1, S, LANES), lambda b, h, i: (b, h, 0, 0))
    return pl.pallas_call(
        functools.partial(_nsa_attn_kernel, tq=tq, tkv=tkv),
        grid=(B, HK, S // tq),
        in_specs=[
            pl.BlockSpec((1, tq, gw), lambda b, h, i: (b, i, h)),
            tab_q, tab_q, tab_q, tab_k, tab_k, tab_k,
            pl.BlockSpec((1, 1, kvcmp.shape[2], LANES), lambda b, h, i: (b, h, 0, 0)),
            kv_spec, kv_spec,
            pl.BlockSpec((1, 1, tq, LANES), lambda b, h, i: (b, h, i, 0)),
        ],
        out_specs=pl.BlockSpec((1, tq, gw), lambda b, h, i: (b, i, h)),
        out_shape=jax.ShapeDtypeStruct((B, S, NSA_Q), BF16),
        scratch_shapes=[pltpu.VMEM((S, LANES), BF16), pltpu.VMEM((S, LANES), BF16)],
        compiler_params=_cparams(("parallel", "parallel", "arbitrary")),
        name="nsa_attention",
    )(q, *tabs, *tabs, kvcmp, kvs, kvw, gates)


def _out_proj_kernel(o_ref, w_ref, g_ref, h_ref, y_ref):
    m = _dot(o_ref[0], w_ref[...])
    y_ref[0] = h_ref[0] + _rms(m, g_ref[...])


def _out_proj(o, w_o, g_post, h, tm=512):
    B, S, D = h.shape
    row = pl.BlockSpec((1, tm, D), lambda b, i: (b, i, 0))
    return pl.pallas_call(
        _out_proj_kernel,
        grid=(B, S // tm),
        in_specs=[row, pl.BlockSpec((D, D), lambda b, i: (0, 0)),
                  pl.BlockSpec((1, D), lambda b, i: (0, 0)), row],
        out_specs=row,
        out_shape=jax.ShapeDtypeStruct((B, S, D), F32),
        compiler_params=_cparams(("parallel", "parallel")),
        name="out_proj",
    )(o, w_o, g_post, h)


def _dil_out_kernel(o0_ref, o1_ref, o2_ref, l0_ref, l1_ref, l2_ref, w_ref, g_ref, h_ref, y_ref):
    lses = (l0_ref[0], l1_ref[0], l2_ref[0])
    mx = jnp.maximum(jnp.maximum(lses[0], lses[1]), lses[2])
    ex = [jnp.exp(l - mx) for l in lses]
    den = ex[0] + ex[1] + ex[2]
    alphas = [e / den for e in ex]
    outs = (o0_ref, o1_ref, o2_ref)
    tm = h_ref.shape[1]
    lo = lax.broadcasted_iota(jnp.int32, (tm, LANES), 1) < HEAD_DIM
    slabs = []
    for c in range(DIL_HEADS // 2):
        acc = jnp.zeros((tm, LANES), F32)
        for n in range(3):
            a = jnp.where(lo, alphas[n][:, 2 * c:2 * c + 1], alphas[n][:, 2 * c + 1:2 * c + 2])
            acc = acc + a * outs[n][0, :, c * LANES:(c + 1) * LANES].astype(F32)
        slabs.append(acc.astype(BF16))
    o = jnp.concatenate(slabs, axis=1)
    y_ref[0] = h_ref[0] + _rms(_dot(o, w_ref[...]), g_ref[...])


def _dil_out(outs, lses, w_o, g_post, h, tm=512):
    B, S, D = h.shape
    row = pl.BlockSpec((1, tm, D), lambda b, i: (b, i, 0))
    lrow = pl.BlockSpec((1, tm, LANES), lambda b, i: (b, i, 0))
    return pl.pallas_call(
        _dil_out_kernel,
        grid=(B, S // tm),
        in_specs=[row, row, row, lrow, lrow, lrow,
                  pl.BlockSpec((D, D), lambda b, i: (0, 0)),
                  pl.BlockSpec((1, D), lambda b, i: (0, 0)), row],
        out_specs=row,
        out_shape=jax.ShapeDtypeStruct((B, S, D), F32),
        compiler_params=_cparams(("parallel", "parallel")),
        name="dil_out_proj",
    )(*outs, *lses, w_o, g_post, h)


def _dil_proj_kernel(h_ref, c_ref, s1_ref, s2_ref, g_ref, w_ref, q_ref, k_ref, v_ref, *, nres):
    D = D_MODEL
    tm = h_ref.shape[1]
    x = jnp.concatenate([h_ref[0, :, r * D:(r + 1) * D] for r in range(nres)], axis=0)
    pick = lambda ref: jnp.concatenate(
        [ref[0, :, r * LANES:(r + 1) * LANES] for r in range(nres)], axis=0)
    c, s1, s2 = pick(c_ref), pick(s1_ref), pick(s2_ref)
    a = _rms(x, g_ref[...]).astype(BF16)
    w = w_ref[...]
    q = _rope_wide(_dot(a, w[:, :D]), c, s1, s2) * SCALE
    k = _rope_wide(_dot(a, w[:, D:2 * D]), c, s1, s2)
    v = _dot(a, w[:, 2 * D:])
    for r in range(nres):
        rows = slice(r * tm, (r + 1) * tm)
        q_ref[0, r] = q[rows].astype(BF16)
        k_ref[0, r] = k[rows].astype(BF16)
        v_ref[0, r] = v[rows].astype(BF16)


def _dil_proj(h, tabs, g_pre, w, dil, tm, nres):
    B, S, D = h.shape
    L = S // dil
    hv = h.reshape(B, L, dil * D)
    tv = [t.reshape(B, L, dil * LANES) for t in tabs]
    out_shape = jax.ShapeDtypeStruct((B, dil, L, D), BF16)
    out_spec = pl.BlockSpec((1, nres, tm, D), lambda b, r, i: (b, r, i, 0))
    tab_spec = pl.BlockSpec((1, tm, nres * LANES), lambda b, r, i: (b, i, r))
    return pl.pallas_call(
        functools.partial(_dil_proj_kernel, nres=nres),
        grid=(B, dil // nres, L // tm),
        in_specs=[
            pl.BlockSpec((1, tm, nres * D), lambda b, r, i: (b, i, r)),
            tab_spec, tab_spec, tab_spec,
            pl.BlockSpec((1, D), lambda b, r, i: (0, 0)),
            pl.BlockSpec((D, 3 * D), lambda b, r, i: (0, 0)),
        ],
        out_specs=[out_spec, out_spec, out_spec],
        out_shape=[out_shape, out_shape, out_shape],
        compiler_params=_cparams(("parallel", "parallel", "parallel")),
        name=f"dil_proj_d{dil}",
    )(hv, *tv, g_pre, w)


def _dil_attn_kernel(q_ref, kp_ref, kc_ref, vp_ref, vc_ref, o_ref, lse_ref, *, band, has_prev):
    tq = DIL_BAND_BLOCK
    i = pl.program_id(2)
    nk = 2 * tq if has_prev else tq
    col = lax.broadcasted_iota(jnp.int32, (tq, nk), 1)
    rel = lax.broadcasted_iota(jnp.int32, (tq, nk), 0) + (nk - tq) - col
    mask = (rel >= 0) & (rel <= band)
    if has_prev:
        mask = mask & ((col >= tq) | (i > 0))
    lane = lax.broadcasted_iota(jnp.int32, (tq, LANES), 1)
    lo = lane < HEAD_DIM
    lse_tile = jnp.zeros((tq, LANES), F32)
    for c in range(DIL_HEADS // 2):
        cols = slice(c * LANES, (c + 1) * LANES)
        q = q_ref[0, 0, :, cols]
        if has_prev:
            k = jnp.concatenate([kp_ref[0, 0, :, cols], kc_ref[0, 0, :, cols]], axis=0)
            v = jnp.concatenate([vp_ref[0, 0, :, cols], vc_ref[0, 0, :, cols]], axis=0)
        else:
            k = kc_ref[0, 0, :, cols]
            v = vc_ref[0, 0, :, cols]
        zero = jnp.zeros_like(q)
        halves = []
        for hh, qh in enumerate((jnp.where(lo, q, zero), jnp.where(lo, zero, q))):
            s = jnp.where(mask, _dot_t(qh, k), NEG)
            m = jnp.max(s, axis=-1, keepdims=True)
            p = jnp.exp(s - m)
            l = jnp.sum(p, axis=-1, keepdims=True)
            halves.append(_dot((p / l).astype(BF16), v))
            lse_tile = jnp.where(lane == 2 * c + hh, m + jnp.log(l), lse_tile)
        o_ref[0, :, cols] = jnp.where(lo, halves[0], halves[1]).astype(o_ref.dtype)
    lse_ref[0] = lse_tile


def _dil_attention(q, k, v, band):
    B, dil, L, D = q.shape
    tq = DIL_BAND_BLOCK
    has_prev = L > tq
    cur = pl.BlockSpec((1, 1, tq, D), lambda b, r, i: (b, r, i, 0))
    prev = pl.BlockSpec((1, 1, tq, D), lambda b, r, i: (b, r, jnp.maximum(i - 1, 0), 0))
    o, lse = pl.pallas_call(
        functools.partial(_dil_attn_kernel, band=band, has_prev=has_prev),
        grid=(B, dil, L // tq),
        in_specs=[cur, prev, cur, prev, cur],
        out_specs=[pl.BlockSpec((1, tq, D), lambda b, r, i: (b, i, r)),
                   pl.BlockSpec((1, tq, LANES), lambda b, r, i: (b, i, r))],
        out_shape=[jax.ShapeDtypeStruct((B, L, dil * D), BF16),
                   jax.ShapeDtypeStruct((B, L, dil * LANES), F32)],
        compiler_params=_cparams(("parallel", "parallel", "arbitrary")),
        name=f"dil_attn_d{dil}",
    )(q, k, k, v, v)
    return o.reshape(B, L * dil, D), lse.reshape(B, L * dil, LANES)


def _ffn_kernel(h_ref, gpre_ref, wg_ref, wu_ref, wd_ref, gpost_ref, y_ref, a_scr, acc_scr):
    j = pl.program_id(2)

    @pl.when(j == 0)
    def _():
        a_scr[...] = _rms(h_ref[0], gpre_ref[...]).astype(BF16)
        acc_scr[...] = jnp.zeros_like(acc_scr)

    a = a_scr[...]
    g = _dot(a, wg_ref[...])
    u = _dot(a, wu_ref[...])
    mid = (g * _sigmoid(g) * u).astype(BF16)
    acc_scr[...] += _dot(mid, wd_ref[...])

    @pl.when(j == pl.num_programs(2) - 1)
    def _():
        y_ref[0] = h_ref[0] + _rms(acc_scr[...], gpost_ref[...])


def _ffn(h, g_pre, w_gu, w_down, g_post, tm=512, tf=1408):
    B, S, D = h.shape
    nf = D_FF // tf
    row = pl.BlockSpec((1, tm, D), lambda b, i, j: (b, i, 0))
    vec = pl.BlockSpec((1, D), lambda b, i, j: (0, 0))
    return pl.pallas_call(
        _ffn_kernel,
        grid=(B, S // tm, nf),
        in_specs=[row, vec,
                  pl.BlockSpec((D, tf), lambda b, i, j: (0, j)),
                  pl.BlockSpec((D, tf), lambda b, i, j: (0, nf + j)),
                  pl.BlockSpec((tf, D), lambda b, i, j: (j, 0)),
                  vec],
        out_specs=row,
        out_shape=jax.ShapeDtypeStruct((B, S, D), F32),
        scratch_shapes=[pltpu.VMEM((tm, D), BF16), pltpu.VMEM((tm, D), F32)],
        compiler_params=_cparams(("parallel", "parallel", "arbitrary")),
        name="swiglu_ffn",
    )(h, g_pre, w_gu, w_gu, w_down, g_post)


def _nsa_layer(h, tabs, g_pre, g_post, w_in, cmp_pos, wk1, wk2, wv1, wv2, w_o):
    q, kvc, kvs, kvw, gates = _nsa_proj(h, g_pre, _nsa_in_weight(w_in))
    kvcmp = _compress(kvc, *_compress_weights(cmp_pos, wk1, wk2, wv1, wv2))
    o = _nsa_attention(q, tabs, kvcmp, kvs, kvw, gates)
    return _out_proj(o, w_o.astype(BF16), g_post, h)


def _dil_layer(h, tabs, g_pre, g_post, w_in, w_o):
    D = D_MODEL
    outs, lses = [], []
    for gi, (window, dil) in enumerate(DIL_PATTERNS):
        w = w_in[:, gi * 3 * D:(gi + 1) * 3 * D].astype(BF16)
        L = h.shape[1] // dil
        tm = min(L, 512)
        nres = max(1, min(dil, 512 // tm))
        q, k, v = _dil_proj(h, tabs, g_pre, w, dil, tm, nres)
        o, lse = _dil_attention(q, k, v, window // dil)
        outs.append(o)
        lses.append(lse)
    return _dil_out(outs, lses, w_o.astype(BF16), g_post, h)


def kernel(x, positions, norm_mix_pre, norm_mix_post, norm_ffn_pre, norm_ffn_post, ffn_w_gu,
           ffn_w_down, nsa_w_in, nsa_cmp_pos, nsa_cmp_wk1, nsa_cmp_wk2, nsa_cmp_wv1,
           nsa_cmp_wv2, nsa_w_o, dil_w_in, dil_w_o):
    depth = norm_mix_pre.shape[0]
    tabs = _rope_tables(positions)
    vec = lambda g: g.reshape(1, D_MODEL)
    h = x
    for i in range(depth):
        j = i // 2
        if i % 2 == 0:
            h = _nsa_layer(h, tabs, vec(norm_mix_pre[i]), vec(norm_mix_post[i]), nsa_w_in[j],
                           nsa_cmp_pos[j], nsa_cmp_wk1[j], nsa_cmp_wk2[j], nsa_cmp_wv1[j],
                           nsa_cmp_wv2[j], nsa_w_o[j])
        else:
            h = _dil_layer(h, tabs, vec(norm_mix_pre[i]), vec(norm_mix_post[i]),
                           dil_w_in[j], dil_w_o[j])
        h = _ffn(h, vec(norm_ffn_pre[i]), ffn_w_gu[i].astype(BF16), ffn_w_down[i].astype(BF16),
                 vec(norm_ffn_post[i]))
    return h
```

```python
import functools

import jax
import jax.numpy as jnp
from jax import lax
from jax.experimental import pallas as pl
from jax.experimental.pallas import tpu as pltpu

F32 = jnp.float32
BF16 = jnp.bfloat16

D_MODEL = 1024
HEAD_DIM = 64
ROPE_DIM = HEAD_DIM // 4
ROPE_HALF = ROPE_DIM // 2
ROPE_THETA = 500000.0
NORM_EPS = 1e-6
NEG = -1e30
SCALE = HEAD_DIM ** -0.5

NSA_HEADS = 16
NSA_KV_HEADS = 4
NSA_GROUP = NSA_HEADS // NSA_KV_HEADS
CMP_BLOCK = 32
CMP_STRIDE = 16
CMP_HIDDEN = 4 * HEAD_DIM
SEL_BLOCK = 64
SEL_SHIFT = 6
SEL_TOPN = 8
FORCE_SCORE = 1e4
WIN = 512
NSA_Q = NSA_HEADS * HEAD_DIM
NSA_KV = NSA_KV_HEADS * 2 * HEAD_DIM
NSA_GATES = 3 * NSA_HEADS
NSA_W_COLS = NSA_Q + 3 * NSA_KV + 128

DIL_PATTERNS = ((128, 1), (512, 4), (2048, 16))
DIL_HEADS = 16
DIL_BAND_BLOCK = 128

D_FF = 2816

LANES = 128
VMEM_LIMIT = 56 * 1024 * 1024


def _cparams(sem):
    return pltpu.CompilerParams(dimension_semantics=sem, vmem_limit_bytes=VMEM_LIMIT)


def _rms(x, g):
    ms = jnp.mean(x * x, axis=-1, keepdims=True)
    return x * lax.rsqrt(ms + NORM_EPS) * g


def _sigmoid(x):
    return 1.0 / (1.0 + jnp.exp(-x))


def _rope128(x, c, s1, s2):
    return x * c + pltpu.roll(x, ROPE_HALF, 1) * s1 + pltpu.roll(x, LANES - ROPE_HALF, 1) * s2


def _rope_wide(x, c, s1, s2):
    n = x.shape[1] // LANES
    return jnp.concatenate(
        [_rope128(x[:, i * LANES:(i + 1) * LANES], c, s1, s2) for i in range(n)], axis=1)


def _dot(a, b):
    return jnp.dot(a, b, preferred_element_type=F32)


def _dot_t(a, b):
    return lax.dot_general(a, b, (((1,), (1,)), ((), ())), preferred_element_type=F32)


def _rope_tables(positions):
    inv = ROPE_THETA ** (-jnp.arange(0, ROPE_DIM, 2, dtype=F32) / ROPE_DIM)
    ang = positions.astype(F32)[..., None] * inv
    cos, sin = jnp.cos(ang), jnp.sin(ang)
    one = jnp.ones(ang.shape[:-1] + (HEAD_DIM - ROPE_DIM,), F32)
    zero = jnp.zeros_like(one)
    z8 = jnp.zeros_like(sin)
    c64 = jnp.concatenate([cos, cos, one], axis=-1)
    s1 = jnp.concatenate([z8, sin, zero], axis=-1)
    s2 = jnp.concatenate([-sin, z8, zero], axis=-1)
    tile = lambda t: jnp.concatenate([t, t], axis=-1)
    lane_tabs = (tile(c64), tile(s1), tile(s2))
    return lane_tabs, cos.transpose(0, 2, 1), sin.transpose(0, 2, 1)


def _nsa_proj_kernel(h_ref, g_ref, wt_ref, wkv_ref, qt_ref, kvc_ref, kvs_ref, kvw_ref, gt_ref):
    a = _rms(h_ref[0], g_ref[...]).astype(BF16)
    rt = _dot_t(wt_ref[...], a)
    qt_ref[0] = (rt[:NSA_Q] * SCALE).astype(BF16)
    gt_ref[0] = _sigmoid(rt[NSA_Q:])
    r = _dot(a, wkv_ref[...])
    for i, ref in enumerate((kvc_ref, kvs_ref, kvw_ref)):
        for hk in range(NSA_KV_HEADS):
            lo = i * NSA_KV + hk * LANES
            ref[0, hk] = r[:, lo:lo + LANES].astype(BF16)


def _nsa_proj(h, g_pre, wt, wkv, tm=512):
    B, S, D = h.shape
    kv_shape = jax.ShapeDtypeStruct((B, NSA_KV_HEADS, S, LANES), BF16)
    kv_spec = pl.BlockSpec((1, NSA_KV_HEADS, tm, LANES), lambda b, i: (b, 0, i, 0))
    return pl.pallas_call(
        _nsa_proj_kernel,
        grid=(B, S // tm),
        in_specs=[
            pl.BlockSpec((1, tm, D), lambda b, i: (b, i, 0)),
            pl.BlockSpec((1, D), lambda b, i: (0, 0)),
            pl.BlockSpec((NSA_Q + LANES, D), lambda b, i: (0, 0)),
            pl.BlockSpec((D, 3 * NSA_KV), lambda b, i: (0, 0)),
        ],
        out_specs=[
            pl.BlockSpec((1, NSA_Q, tm), lambda b, i: (b, 0, i)),
            kv_spec, kv_spec, kv_spec,
            pl.BlockSpec((1, LANES, tm), lambda b, i: (b, 0, i)),
        ],
        out_shape=[
            jax.ShapeDtypeStruct((B, NSA_Q, S), BF16),
            kv_shape, kv_shape, kv_shape,
            jax.ShapeDtypeStruct((B, LANES, S), F32),
        ],
        compiler_params=_cparams(("parallel", "parallel")),
        name="nsa_proj",
    )(h, g_pre, wt, wkv)


def _nsa_in_weight(w_in):
    dq, dkv = NSA_Q, NSA_KV_HEADS * HEAD_DIM
    kv = w_in[:, dq:dq + 6 * dkv].reshape(D_MODEL, 3, 2, NSA_KV_HEADS, HEAD_DIM)
    kv = kv.transpose(0, 1, 3, 2, 4).reshape(D_MODEL, 3 * NSA_KV)
    gates = jnp.pad(w_in[:, dq + 6 * dkv:], ((0, 0), (0, LANES - NSA_GATES)))
    wt = jnp.concatenate([w_in[:, :dq], gates], axis=1).T
    return wt.astype(BF16), kv.astype(BF16)


N_CHUNK_TOK = CMP_STRIDE
CHUNK_COLS = N_CHUNK_TOK * 2 * HEAD_DIM


def _compress_kernel(x_ref, pa_ref, pb_ref, wa_ref, wb_ref, w2_ref, o_ref):
    x = x_ref[0, 0].astype(F32)
    ha = _dot((x + pa_ref[...]).astype(BF16), wa_ref[...])
    hb = _dot((x + pb_ref[...]).astype(BF16), wb_ref[...])
    n = x.shape[0]
    hid = ha + pltpu.roll(hb, n - 1, 0)
    act = hid * _sigmoid(hid)
    o_ref[0, 0] = _dot(act.astype(BF16), w2_ref[...])


def _compress(kvc, pa, pb, wa, wb, w2):
    B, HK, S, _ = kvc.shape
    nchunk = S // N_CHUNK_TOK
    x = kvc.reshape(B, HK, nchunk, CHUNK_COLS)
    full = lambda shape: pl.BlockSpec(shape, lambda b, h: (0,) * len(shape))
    return pl.pallas_call(
        _compress_kernel,
        grid=(B, HK),
        in_specs=[
            pl.BlockSpec((1, 1, nchunk, CHUNK_COLS), lambda b, h: (b, h, 0, 0)),
            full((1, CHUNK_COLS)), full((1, CHUNK_COLS)),
            full((CHUNK_COLS, 2 * CMP_HIDDEN)), full((CHUNK_COLS, 2 * CMP_HIDDEN)),
            full((2 * CMP_HIDDEN, LANES)),
        ],
        out_specs=pl.BlockSpec((1, 1, nchunk, LANES), lambda b, h: (b, h, 0, 0)),
        out_shape=jax.ShapeDtypeStruct((B, HK, nchunk, LANES), F32),
        compiler_params=_cparams(("parallel", "parallel")),
        name="nsa_compress",
    )(x, pa, pb, wa, wb, w2)


def _compress_weights(cmp_pos, wk1, wk2, wv1, wv2):
    half = N_CHUNK_TOK * HEAD_DIM

    def first_layer(lo):
        wk = wk1[lo:lo + half].reshape(N_CHUNK_TOK, HEAD_DIM, CMP_HIDDEN)
        wv = wv1[lo:lo + half].reshape(N_CHUNK_TOK, HEAD_DIM, CMP_HIDDEN)
        z = jnp.zeros_like(wk)
        w = jnp.stack([jnp.concatenate([wk, z], -1), jnp.concatenate([z, wv], -1)], axis=1)
        return w.reshape(CHUNK_COLS, 2 * CMP_HIDDEN).astype(BF16)

    def pos(lo):
        p = cmp_pos[lo:lo + N_CHUNK_TOK]
        return jnp.stack([p, p], axis=1).reshape(1, CHUNK_COLS)

    z2 = jnp.zeros_like(wk2)
    w2 = jnp.concatenate([jnp.concatenate([wk2, z2], 1), jnp.concatenate([z2, wv2], 1)], 0)
    return pos(0), pos(N_CHUNK_TOK), first_layer(0), first_layer(half), w2.astype(BF16)


def _flash_t(qst, k_scr, kt_scr, j_lo, j_hi, mask_fn, tq, tkv):
    G = NSA_GROUP

    def body(j, carry):
        m, l, acc = carry
        kv = k_scr[pl.ds(pl.multiple_of(j * tkv, tkv), tkv), :]
        s = _dot(kv, qst)
        mask = mask_fn(j)
        ms, ps = [], []
        for g in range(G):
            sl = slice(g * tq, (g + 1) * tq)
            sg = jnp.where(mask, s[:, sl], NEG)
            mg = jnp.maximum(m[:, sl], jnp.max(sg, axis=0, keepdims=True))
            ms.append(mg)
            ps.append(jnp.exp(sg - mg))
        m_new = jnp.concatenate(ms, axis=1)
        p = jnp.concatenate(ps, axis=1)
        a = jnp.exp(m - m_new)
        l = a * l + jnp.sum(p, axis=0, keepdims=True)
        acc = a * acc + _dot(kt_scr[j, HEAD_DIM:, :], p.astype(BF16))
        return m_new, l, acc

    init = (jnp.full((1, G * tq), NEG, F32), jnp.zeros((1, G * tq), F32),
            jnp.zeros((HEAD_DIM, G * tq), F32))
    _, l, acc = lax.fori_loop(j_lo, j_hi, body, init)
    o = acc / l
    return [o[:, g * tq:(g + 1) * tq] for g in range(G)]


def _nsa_attn_kernel(qt_ref, cos_ref, sin_ref, kc_ref, ks1_ref, ks2_ref, kvc_ref, kvs_ref,
                     kvw_ref, gt_ref, o_ref, ks_scr, kst_scr, kw_scr, kwt_scr, *, tq, tkv):
    G = NSA_GROUP
    hk = pl.program_id(1)
    qi = pl.program_id(2)
    S = kvs_ref.shape[2]
    n_sel = S // SEL_BLOCK

    @pl.when(qi == 0)
    def _():
        is_key = lax.broadcasted_iota(jnp.int32, (S, LANES), 1) < HEAD_DIM
        c = jnp.where(is_key, kc_ref[0], 1.0)
        s1 = jnp.where(is_key, ks1_ref[0], 0.0)
        s2 = jnp.where(is_key, ks2_ref[0], 0.0)
        for src, scr, scr_t in ((kvs_ref, ks_scr, kst_scr), (kvw_ref, kw_scr, kwt_scr)):
            kr = _rope128(src[0, 0].astype(F32), c, s1, s2)
            scr[...] = kr.astype(BF16)
            for jj in range(S // tkv):
                scr_t[jj] = kr[jj * tkv:(jj + 1) * tkv].T.astype(BF16)

    q = qt_ref[0].astype(F32)
    cos, sin = cos_ref[0], sin_ref[0]
    zpad = jnp.zeros((LANES - HEAD_DIM, tq), F32)
    plain, rot = [], []
    for g in range(G):
        x = q[g * HEAD_DIM:(g + 1) * HEAD_DIM]
        x1, x2 = x[:ROPE_HALF], x[ROPE_HALF:ROPE_DIM]
        xr = jnp.concatenate([x1 * cos - x2 * sin, x2 * cos + x1 * sin, x[ROPE_DIM:]], axis=0)
        plain.append(jnp.concatenate([x, zpad], axis=0))
        rot.append(jnp.concatenate([xr, zpad], axis=0))
    qst_plain = jnp.concatenate(plain, axis=1).astype(BF16)
    qst_rot = jnp.concatenate(rot, axis=1).astype(BF16)

    kvc = kvc_ref[0, 0]
    c_idx = lax.broadcasted_iota(jnp.int32, (LANES, tq), 0)
    t_pos = qi * tq + lax.broadcasted_iota(jnp.int32, (LANES, tq), 1)
    c_valid = (CMP_STRIDE * c_idx + CMP_BLOCK - 1) <= t_pos
    c_valid_f = jnp.where(c_valid, 1.0, 0.0)
    s = _dot(kvc.astype(BF16), qst_plain)
    ps = []
    for g in range(G):
        sg = jnp.where(c_valid, s[:, g * tq:(g + 1) * tq], NEG)
        e = jnp.exp(sg - jnp.max(sg, axis=0, keepdims=True))
        ps.append(e / jnp.sum(e, axis=0, keepdims=True) * c_valid_f)
    o_cmp = _dot(kvc.T.astype(BF16), jnp.concatenate(ps, axis=1).astype(BF16))

    p_sum = ps[0]
    for g in range(1, G):
        p_sum = p_sum + ps[g]
    jb0 = lax.broadcasted_iota(jnp.int32, (LANES, LANES), 0) * SEL_BLOCK
    cb = lax.broadcasted_iota(jnp.int32, (LANES, LANES), 1) * CMP_STRIDE
    overlap = jnp.where((cb < jb0 + SEL_BLOCK) & (cb + CMP_BLOCK > jb0), 1.0, 0.0).astype(BF16)
    p_hi = p_sum.astype(BF16)
    p_lo = (p_sum - p_hi.astype(F32)).astype(BF16)
    imp = (_dot(overlap, p_hi) + _dot(overlap, p_lo))[:n_sel]
    jrow = lax.broadcasted_iota(jnp.int32, (n_sel, tq), 0)
    tb = jnp.right_shift(qi * tq + lax.broadcasted_iota(jnp.int32, (n_sel, tq), 1), SEL_SHIFT)
    forced = (jrow == 0) | (jrow == tb) | (jrow == tb - 1)
    score = jnp.where(forced, FORCE_SCORE, jnp.where(jrow <= tb, imp, -1.0))
    rank = jnp.zeros((n_sel, tq), F32)
    for j2 in range(n_sel):
        other = score[j2:j2 + 1, :]
        beats = (other > score) | ((other == score) & (jrow > j2))
        rank = rank + jnp.where(beats, 1.0, 0.0)
    chosen = jnp.where(rank < min(SEL_TOPN, n_sel), 1.0, 0.0).astype(BF16)

    key_r = lax.broadcasted_iota(jnp.int32, (tkv, tq), 0)
    qry = qi * tq + lax.broadcasted_iota(jnp.int32, (tkv, tq), 1)
    e_key = lax.broadcasted_iota(jnp.int32, (tkv, n_sel), 0)
    e_blk = lax.broadcasted_iota(jnp.int32, (tkv, n_sel), 1)

    def sel_mask(j):
        expand = jnp.where(jnp.right_shift(j * tkv + e_key, SEL_SHIFT) == e_blk, 1.0, 0.0)
        member = _dot(expand.astype(BF16), chosen)
        return (member > 0.5) & (j * tkv + key_r <= qry)

    def win_mask(j):
        d = qry - (j * tkv + key_r)
        return (d >= 0) & (d <= WIN - 1)

    hi_chunk = (qi * tq + tq + tkv - 1) // tkv
    o_sel = _flash_t(qst_rot, ks_scr, kst_scr, 0, hi_chunk, sel_mask, tq, tkv)
    lo_chunk = jnp.maximum(qi * tq - (WIN - 1), 0) // tkv
    o_win = _flash_t(qst_rot, kw_scr, kwt_scr, lo_chunk, hi_chunk, win_mask, tq, tkv)

    heads = []
    for g in range(G):
        sl = slice(g * tq, (g + 1) * tq)
        gate = lambda i: gt_ref[0, pl.ds(i * NSA_HEADS + hk * G + g, 1), :]
        heads.append(gate(0) * o_cmp[HEAD_DIM:, sl] + gate(1) * o_sel[g] + gate(2) * o_win[g])
    o_ref[0] = jnp.concatenate(heads, axis=0).T.astype(o_ref.dtype)


def _nsa_attention(qt, cos_t, sin_t, tabs, kvcmp, kvs, kvw, gt, tq=256, tkv=256):
    B, _, S = qt.shape
    HK = NSA_KV_HEADS
    gw = NSA_GROUP * HEAD_DIM
    tab_q = pl.BlockSpec((1, ROPE_HALF, tq), lambda b, h, i: (b, 0, i))
    tab_k = pl.BlockSpec((1, S, LANES), lambda b, h, i: (b, 0, 0))
    kv_spec = pl.BlockSpec((1, 1, S, LANES), lambda b, h, i: (b, h, 0, 0))
    kv_scr = pltpu.VMEM((S, LANES), BF16)
    kvt_scr = pltpu.VMEM((S // tkv, LANES, tkv), BF16)
    return pl.pallas_call(
        functools.partial(_nsa_attn_kernel, tq=tq, tkv=tkv),
        grid=(B, HK, S // tq),
        in_specs=[
            pl.BlockSpec((1, gw, tq), lambda b, h, i: (b, h, i)),
            tab_q, tab_q, tab_k, tab_k, tab_k,
            pl.BlockSpec((1, 1, kvcmp.shape[2], LANES), lambda b, h, i: (b, h, 0, 0)),
            kv_spec, kv_spec,
            pl.BlockSpec((1, LANES, tq), lambda b, h, i: (b, 0, i)),
        ],
        out_specs=pl.BlockSpec((1, tq, gw), lambda b, h, i: (b, i, h)),
        out_shape=jax.ShapeDtypeStruct((B, S, NSA_Q), BF16),
        scratch_shapes=[kv_scr, kvt_scr, kv_scr, kvt_scr],
        compiler_params=_cparams(("parallel", "parallel", "arbitrary")),
        name="nsa_attention",
    )(qt, cos_t, sin_t, *tabs, kvcmp, kvs, kvw, gt)


def _out_proj_kernel(o_ref, w_ref, g_ref, h_ref, y_ref):
    m = _dot(o_ref[0], w_ref[...])
    y_ref[0] = h_ref[0] + _rms(m, g_ref[...])


def _out_proj(o, w_o, g_post, h, tm=512):
    B, S, D = h.shape
    row = pl.BlockSpec((1, tm, D), lambda b, i: (b, i, 0))
    return pl.pallas_call(
        _out_proj_kernel,
        grid=(B, S // tm),
        in_specs=[row, pl.BlockSpec((D, D), lambda b, i: (0, 0)),
                  pl.BlockSpec((1, D), lambda b, i: (0, 0)), row],
        out_specs=row,
        out_shape=jax.ShapeDtypeStruct((B, S, D), F32),
        compiler_params=_cparams(("parallel", "parallel")),
        name="out_proj",
    )(o, w_o, g_post, h)


def _dil_out_kernel(o0_ref, o1_ref, o2_ref, l0_ref, l1_ref, l2_ref, w_ref, g_ref, h_ref, y_ref,
                    *scratch):
    for ref, scr in zip((o0_ref, o1_ref, o2_ref, l0_ref, l1_ref, l2_ref), scratch):
        dil, rows = ref.shape[1], ref.shape[2]
        for r in range(dil):
            rows_r = slice(None) if dil == 1 else pl.ds(r, rows, stride=dil)
            for ct in range(ref.shape[3] // LANES):
                scr[ct, rows_r, :] = ref[0, r, :, ct * LANES:(ct + 1) * LANES].astype(F32)
    outs, lses = scratch[:3], [s[0] for s in scratch[3:]]
    mx = jnp.maximum(jnp.maximum(lses[0], lses[1]), lses[2])
    ex = [jnp.exp(l - mx) for l in lses]
    den = ex[0] + ex[1] + ex[2]
    alphas = [e / den for e in ex]
    tm = h_ref.shape[1]
    lo = lax.broadcasted_iota(jnp.int32, (tm, LANES), 1) < HEAD_DIM
    slabs = []
    for c in range(DIL_HEADS // 2):
        acc = jnp.zeros((tm, LANES), F32)
        for n in range(3):
            a = jnp.where(lo, alphas[n][:, 2 * c:2 * c + 1], alphas[n][:, 2 * c + 1:2 * c + 2])
            acc = acc + a * outs[n][c]
        slabs.append(acc.astype(BF16))
    o = jnp.concatenate(slabs, axis=1)
    y_ref[0] = h_ref[0] + _rms(_dot(o, w_ref[...]), g_ref[...])


def _dil_out(outs, lses, w_o, g_post, h, tm=512):
    B, S, D = h.shape
    row = pl.BlockSpec((1, tm, D), lambda b, i: (b, i, 0))
    cls = lambda x: pl.BlockSpec((1, x.shape[1], tm // x.shape[1], x.shape[3]),
                                 lambda b, i: (b, 0, i, 0))
    return pl.pallas_call(
        _dil_out_kernel,
        grid=(B, S // tm),
        in_specs=[cls(x) for x in (*outs, *lses)] + [
            pl.BlockSpec((D, D), lambda b, i: (0, 0)),
            pl.BlockSpec((1, D), lambda b, i: (0, 0)), row],
        out_specs=row,
        out_shape=jax.ShapeDtypeStruct((B, S, D), F32),
        scratch_shapes=([pltpu.VMEM((D // LANES, tm, LANES), F32)] * 3
                        + [pltpu.VMEM((1, tm, LANES), F32)] * 3),
        compiler_params=_cparams(("parallel", "parallel")),
        name="dil_out_proj",
    )(*outs, *lses, w_o, g_post, h)


def _class_major(ref, dil):
    rows = ref.shape[0] // dil
    return jnp.concatenate([ref[pl.ds(r, rows, stride=dil), :] for r in range(dil)], axis=0)


def _dil_proj_kernel(h_ref, c_ref, s1_ref, s2_ref, g_ref, w_ref, q_ref, k_ref, v_ref, x_scr,
                     *, dil):
    D = D_MODEL
    rows = h_ref.shape[1] // dil
    if dil == 1:
        x = h_ref[0]
        c, s1, s2 = c_ref[0], s1_ref[0], s2_ref[0]
    else:
        for ct in range(D // LANES):
            x_scr[ct] = h_ref[0, :, ct * LANES:(ct + 1) * LANES]
        x = jnp.concatenate([_class_major(x_scr.at[ct], dil) for ct in range(D // LANES)],
                            axis=1)
        c, s1, s2 = (_class_major(ref.at[0], dil) for ref in (c_ref, s1_ref, s2_ref))
    a = _rms(x, g_ref[...]).astype(BF16)
    w = w_ref[...]
    q = _rope_wide(_dot(a, w[:, :D]), c, s1, s2) * SCALE
    k = _rope_wide(_dot(a, w[:, D:2 * D]), c, s1, s2)
    v = _dot(a, w[:, 2 * D:])
    for r in range(dil):
        sl = slice(r * rows, (r + 1) * rows)
        q_ref[0, r] = q[sl].astype(BF16)
        k_ref[0, r] = k[sl].astype(BF16)
        v_ref[0, r] = v[sl].astype(BF16)


def _dil_proj(h, tabs, g_pre, w, dil, tm=512):
    B, S, D = h.shape
    out_shape = jax.ShapeDtypeStruct((B, dil, S // dil, D), BF16)
    out_spec = pl.BlockSpec((1, dil, tm // dil, D), lambda b, i: (b, 0, i, 0))
    tab_spec = pl.BlockSpec((1, tm, LANES), lambda b, i: (b, i, 0))
    return pl.pallas_call(
        functools.partial(_dil_proj_kernel, dil=dil),
        grid=(B, S // tm),
        in_specs=[
            pl.BlockSpec((1, tm, D), lambda b, i: (b, i, 0)),
            tab_spec, tab_spec, tab_spec,
            pl.BlockSpec((1, D), lambda b, i: (0, 0)),
            pl.BlockSpec((D, 3 * D), lambda b, i: (0, 0)),
        ],
        out_specs=[out_spec, out_spec, out_spec],
        out_shape=[out_shape, out_shape, out_shape],
        scratch_shapes=[pltpu.VMEM((D // LANES, tm, LANES), F32)],
        compiler_params=_cparams(("parallel", "parallel")),
        name=f"dil_proj_d{dil}",
    )(h, *tabs, g_pre, w)


def _dil_attn_kernel(q_ref, kp_ref, kc_ref, vp_ref, vc_ref, o_ref, lse_ref, *, band, has_prev):
    tq = DIL_BAND_BLOCK
    i = pl.program_id(2)
    nk = 2 * tq if has_prev else tq
    col = lax.broadcasted_iota(jnp.int32, (tq, nk), 1)
    rel = lax.broadcasted_iota(jnp.int32, (tq, nk), 0) + (nk - tq) - col
    mask = (rel >= 0) & (rel <= band)
    if has_prev:
        mask = mask & ((col >= tq) | (i > 0))
    lane = lax.broadcasted_iota(jnp.int32, (tq, LANES), 1)
    lo = lane < HEAD_DIM
    lse_tile = jnp.zeros((tq, LANES), F32)
    for c in range(DIL_HEADS // 2):
        cols = slice(c * LANES, (c + 1) * LANES)
        q = q_ref[0, 0, :, cols]
        if has_prev:
            k = jnp.concatenate([kp_ref[0, 0, :, cols], kc_ref[0, 0, :, cols]], axis=0)
            v = jnp.concatenate([vp_ref[0, 0, :, cols], vc_ref[0, 0, :, cols]], axis=0)
        else:
            k = kc_ref[0, 0, :, cols]
            v = vc_ref[0, 0, :, cols]
        zero = jnp.zeros_like(q)
        halves = []
        for hh, qh in enumerate((jnp.where(lo, q, zero), jnp.where(lo, zero, q))):
            s = jnp.where(mask, _dot_t(qh, k), NEG)
            m = jnp.max(s, axis=-1, keepdims=True)
            p = jnp.exp(s - m)
            l = jnp.sum(p, axis=-1, keepdims=True)
            halves.append(_dot((p / l).astype(BF16), v))
            lse_tile = jnp.where(lane == 2 * c + hh, m + jnp.log(l), lse_tile)
        o_ref[0, 0, :, cols] = jnp.where(lo, halves[0], halves[1]).astype(o_ref.dtype)
    lse_ref[0, 0] = lse_tile


def _dil_attention(q, k, v, band):
    B, dil, L, D = q.shape
    tq = DIL_BAND_BLOCK
    has_prev = L > tq
    cur = pl.BlockSpec((1, 1, tq, D), lambda b, r, i: (b, r, i, 0))
    prev = pl.BlockSpec((1, 1, tq, D), lambda b, r, i: (b, r, jnp.maximum(i - 1, 0), 0))
    return pl.pallas_call(
        functools.partial(_dil_attn_kernel, band=band, has_prev=has_prev),
        grid=(B, dil, L // tq),
        in_specs=[cur, prev, cur, prev, cur],
        out_specs=[cur, pl.BlockSpec((1, 1, tq, LANES), lambda b, r, i: (b, r, i, 0))],
        out_shape=[jax.ShapeDtypeStruct((B, dil, L, D), BF16),
                   jax.ShapeDtypeStruct((B, dil, L, LANES), F32)],
        compiler_params=_cparams(("parallel", "parallel", "arbitrary")),
        name=f"dil_attn_d{dil}",
    )(q, k, k, v, v)


def _ffn_kernel(h_ref, gpre_ref, wg_ref, wu_ref, wd_ref, gpost_ref, y_ref, a_scr, acc_scr):
    j = pl.program_id(2)

    @pl.when(j == 0)
    def _():
        a_scr[...] = _rms(h_ref[0], gpre_ref[...]).astype(BF16)
        acc_scr[...] = jnp.zeros_like(acc_scr)

    a = a_scr[...]
    g = _dot(a, wg_ref[...])
    u = _dot(a, wu_ref[...])
    mid = (g * _sigmoid(g) * u).astype(BF16)
    acc_scr[...] += _dot(mid, wd_ref[...])

    @pl.when(j == pl.num_programs(2) - 1)
    def _():
        y_ref[0] = h_ref[0] + _rms(acc_scr[...], gpost_ref[...])


def _ffn(h, g_pre, w_gu, w_down, g_post, tm=512, tf=1408):
    B, S, D = h.shape
    nf = D_FF // tf
    row = pl.BlockSpec((1, tm, D), lambda b, i, j: (b, i, 0))
    vec = pl.BlockSpec((1, D), lambda b, i, j: (0, 0))
    return pl.pallas_call(
        _ffn_kernel,
        grid=(B, S // tm, nf),
        in_specs=[row, vec,
                  pl.BlockSpec((D, tf), lambda b, i, j: (0, j)),
                  pl.BlockSpec((D, tf), lambda b, i, j: (0, nf + j)),
                  pl.BlockSpec((tf, D), lambda b, i, j: (j, 0)),
                  vec],
        out_specs=row,
        out_shape=jax.ShapeDtypeStruct((B, S, D), F32),
        scratch_shapes=[pltpu.VMEM((tm, D), BF16), pltpu.VMEM((tm, D), F32)],
        compiler_params=_cparams(("parallel", "parallel", "arbitrary")),
        name="swiglu_ffn",
    )(h, g_pre, w_gu, w_gu, w_down, g_post)


def _nsa_layer(h, rope, g_pre, g_post, w_in, cmp_pos, wk1, wk2, wv1, wv2, w_o):
    tabs, cos_t, sin_t = rope
    qt, kvc, kvs, kvw, gt = _nsa_proj(h, g_pre, *_nsa_in_weight(w_in))
    kvcmp = _compress(kvc, *_compress_weights(cmp_pos, wk1, wk2, wv1, wv2))
    o = _nsa_attention(qt, cos_t, sin_t, tabs, kvcmp, kvs, kvw, gt)
    return _out_proj(o, w_o.astype(BF16), g_post, h)


def _dil_layer(h, tabs, g_pre, g_post, w_in, w_o):
    D = D_MODEL
    outs, lses = [], []
    for gi, (window, dil) in enumerate(DIL_PATTERNS):
        w = w_in[:, gi * 3 * D:(gi + 1) * 3 * D].astype(BF16)
        q, k, v = _dil_proj(h, tabs, g_pre, w, dil)
        o, lse = _dil_attention(q, k, v, window // dil)
        outs.append(o)
        lses.append(lse)
    return _dil_out(outs, lses, w_o.astype(BF16), g_post, h)


def kernel(x, positions, norm_mix_pre, norm_mix_post, norm_ffn_pre, norm_ffn_post, ffn_w_gu,
           ffn_w_down, nsa_w_in, nsa_cmp_pos, nsa_cmp_wk1, nsa_cmp_wk2, nsa_cmp_wv1,
           nsa_cmp_wv2, nsa_w_o, dil_w_in, dil_w_o):
    depth = norm_mix_pre.shape[0]
    rope = _rope_tables(positions)
    tabs = rope[0]
    vec = lambda g: g.reshape(1, D_MODEL)
    h = x
    for i in range(depth):
        j = i // 2
        if i % 2 == 0:
            h = _nsa_layer(h, rope, vec(norm_mix_pre[i]), vec(norm_mix_post[i]), nsa_w_in[j],
                           nsa_cmp_pos[j], nsa_cmp_wk1[j], nsa_cmp_wk2[j], nsa_cmp_wv1[j],
                           nsa_cmp_wv2[j], nsa_w_o[j])
        else:
            h = _dil_layer(h, tabs, vec(norm_mix_pre[i]), vec(norm_mix_post[i]),
                           dil_w_in[j], dil_w_o[j])
        h = _ffn(h, vec(norm_ffn_pre[i]), ffn_w_gu[i].astype(BF16), ffn_w_down[i].astype(BF16),
                 vec(norm_ffn_post[i]))
    return h
```

```python
import functools

import jax
import jax.numpy as jnp
from jax import lax
from jax.experimental import pallas as pl
from jax.experimental.pallas import tpu as pltpu

F32 = jnp.float32
BF16 = jnp.bfloat16

D_MODEL = 1024
HEAD_DIM = 64
ROPE_DIM = HEAD_DIM // 4
ROPE_HALF = ROPE_DIM // 2
ROPE_THETA = 500000.0
NORM_EPS = 1e-6
NEG = -1e30
SCALE = HEAD_DIM ** -0.5
LOG2E = 1.4426950408889634
LN2 = 0.6931471805599453

NSA_HEADS = 16
NSA_KV_HEADS = 4
NSA_GROUP = NSA_HEADS // NSA_KV_HEADS
CMP_BLOCK = 32
CMP_STRIDE = 16
CMP_HIDDEN = 4 * HEAD_DIM
SEL_BLOCK = 64
SEL_SHIFT = 6
SEL_TOPN = 8
FORCE_SCORE = 1e4
WIN = 512
NSA_Q = NSA_HEADS * HEAD_DIM
NSA_KV = NSA_KV_HEADS * 2 * HEAD_DIM
NSA_GATES = 3 * NSA_HEADS
NSA_W_COLS = NSA_Q + 3 * NSA_KV + 128

DIL_PATTERNS = ((128, 1), (512, 4), (2048, 16))
DIL_HEADS = 16
DIL_BAND_BLOCK = 128

D_FF = 2816

LANES = 128
SUBLANES = 8
VMEM_LIMIT = 56 * 1024 * 1024


def _cparams(sem):
    return pltpu.CompilerParams(dimension_semantics=sem, vmem_limit_bytes=VMEM_LIMIT)


def _rms(x, g):
    ms = jnp.mean(x * x, axis=-1, keepdims=True)
    return x * lax.rsqrt(ms + NORM_EPS) * g


def _sigmoid(x):
    return 1.0 / (1.0 + jnp.exp(-x))


def _rope128(x, c, s1, s2):
    return x * c + pltpu.roll(x, ROPE_HALF, 1) * s1 + pltpu.roll(x, LANES - ROPE_HALF, 1) * s2


def _rope_wide(x, c, s1, s2):
    n = x.shape[1] // LANES
    return jnp.concatenate(
        [_rope128(x[:, i * LANES:(i + 1) * LANES], c, s1, s2) for i in range(n)], axis=1)


def _dot(a, b):
    return jnp.dot(a, b, preferred_element_type=F32)


def _dot_t(a, b):
    return lax.dot_general(a, b, (((1,), (1,)), ((), ())), preferred_element_type=F32)


def _rope_tables(positions):
    inv = ROPE_THETA ** (-jnp.arange(0, ROPE_DIM, 2, dtype=F32) / ROPE_DIM)
    ang = positions.astype(F32)[..., None] * inv
    cos, sin = jnp.cos(ang), jnp.sin(ang)
    one = jnp.ones(ang.shape[:-1] + (HEAD_DIM - ROPE_DIM,), F32)
    zero = jnp.zeros_like(one)
    z8 = jnp.zeros_like(sin)
    c64 = jnp.concatenate([cos, cos, one], axis=-1)
    s1 = jnp.concatenate([z8, sin, zero], axis=-1)
    s2 = jnp.concatenate([-sin, z8, zero], axis=-1)
    tile = lambda t: jnp.concatenate([t, t], axis=-1)
    lane_tabs = (tile(c64), tile(s1), tile(s2))
    return lane_tabs, cos.transpose(0, 2, 1), sin.transpose(0, 2, 1)


def _nsa_proj_kernel(h_ref, g_ref, wt_ref, wkv_ref, qt_ref, kvc_ref, kvs_ref, kvw_ref, gt_ref):
    a = _rms(h_ref[0], g_ref[...]).astype(BF16)
    rt = _dot_t(wt_ref[...], a)
    qt_ref[0] = (rt[:NSA_Q] * (SCALE * LOG2E)).astype(BF16)
    gt_ref[0] = _sigmoid(rt[NSA_Q:])
    r = _dot(a, wkv_ref[...])
    for i, ref in enumerate((kvc_ref, kvs_ref, kvw_ref)):
        for hk in range(NSA_KV_HEADS):
            lo = i * NSA_KV + hk * LANES
            ref[0, hk] = r[:, lo:lo + LANES].astype(BF16)


def _nsa_proj(h, g_pre, wt, wkv, tm=512):
    B, S, D = h.shape
    kv_shape = jax.ShapeDtypeStruct((B, NSA_KV_HEADS, S, LANES), BF16)
    kv_spec = pl.BlockSpec((1, NSA_KV_HEADS, tm, LANES), lambda b, i: (b, 0, i, 0))
    return pl.pallas_call(
        _nsa_proj_kernel,
        grid=(B, S // tm),
        in_specs=[
            pl.BlockSpec((1, tm, D), lambda b, i: (b, i, 0)),
            pl.BlockSpec((1, D), lambda b, i: (0, 0)),
            pl.BlockSpec((NSA_Q + LANES, D), lambda b, i: (0, 0)),
            pl.BlockSpec((D, 3 * NSA_KV), lambda b, i: (0, 0)),
        ],
        out_specs=[
            pl.BlockSpec((1, NSA_Q, tm), lambda b, i: (b, 0, i)),
            kv_spec, kv_spec, kv_spec,
            pl.BlockSpec((1, LANES, tm), lambda b, i: (b, 0, i)),
        ],
        out_shape=[
            jax.ShapeDtypeStruct((B, NSA_Q, S), BF16),
            kv_shape, kv_shape, kv_shape,
            jax.ShapeDtypeStruct((B, LANES, S), F32),
        ],
        compiler_params=_cparams(("parallel", "parallel")),
        name="nsa_proj",
    )(h, g_pre, wt, wkv)


def _nsa_in_weight(w_in):
    dq, dkv = NSA_Q, NSA_KV_HEADS * HEAD_DIM
    kv = w_in[:, dq:dq + 6 * dkv].reshape(D_MODEL, 3, 2, NSA_KV_HEADS, HEAD_DIM)
    kv = kv.transpose(0, 1, 3, 2, 4).reshape(D_MODEL, 3 * NSA_KV)
    gates = jnp.pad(w_in[:, dq + 6 * dkv:], ((0, 0), (0, LANES - NSA_GATES)))
    wt = jnp.concatenate([w_in[:, :dq], gates], axis=1).T
    return wt.astype(BF16), kv.astype(BF16)


N_CHUNK_TOK = CMP_STRIDE
CHUNK_COLS = N_CHUNK_TOK * 2 * HEAD_DIM


def _compress_kernel(x_ref, pa_ref, pb_ref, wa_ref, wb_ref, w2_ref, o_ref):
    x = x_ref[0, 0].astype(F32)
    ha = _dot((x + pa_ref[...]).astype(BF16), wa_ref[...])
    hb = _dot((x + pb_ref[...]).astype(BF16), wb_ref[...])
    n = x.shape[0]
    hid = ha + pltpu.roll(hb, n - 1, 0)
    act = hid * _sigmoid(hid)
    o_ref[0, 0] = _dot(act.astype(BF16), w2_ref[...])


def _compress(kvc, pa, pb, wa, wb, w2):
    B, HK, S, _ = kvc.shape
    nchunk = S // N_CHUNK_TOK
    x = kvc.reshape(B, HK, nchunk, CHUNK_COLS)
    full = lambda shape: pl.BlockSpec(shape, lambda b, h: (0,) * len(shape))
    return pl.pallas_call(
        _compress_kernel,
        grid=(B, HK),
        in_specs=[
            pl.BlockSpec((1, 1, nchunk, CHUNK_COLS), lambda b, h: (b, h, 0, 0)),
            full((1, CHUNK_COLS)), full((1, CHUNK_COLS)),
            full((CHUNK_COLS, 2 * CMP_HIDDEN)), full((CHUNK_COLS, 2 * CMP_HIDDEN)),
            full((2 * CMP_HIDDEN, LANES)),
        ],
        out_specs=pl.BlockSpec((1, 1, nchunk, LANES), lambda b, h: (b, h, 0, 0)),
        out_shape=jax.ShapeDtypeStruct((B, HK, nchunk, LANES), F32),
        compiler_params=_cparams(("parallel", "parallel")),
        name="nsa_compress",
    )(x, pa, pb, wa, wb, w2)


def _compress_weights(cmp_pos, wk1, wk2, wv1, wv2):
    half = N_CHUNK_TOK * HEAD_DIM

    def first_layer(lo):
        wk = wk1[lo:lo + half].reshape(N_CHUNK_TOK, HEAD_DIM, CMP_HIDDEN)
        wv = wv1[lo:lo + half].reshape(N_CHUNK_TOK, HEAD_DIM, CMP_HIDDEN)
        z = jnp.zeros_like(wk)
        w = jnp.stack([jnp.concatenate([wk, z], -1), jnp.concatenate([z, wv], -1)], axis=1)
        return w.reshape(CHUNK_COLS, 2 * CMP_HIDDEN).astype(BF16)

    def pos(lo):
        p = cmp_pos[lo:lo + N_CHUNK_TOK]
        return jnp.stack([p, p], axis=1).reshape(1, CHUNK_COLS)

    z2 = jnp.zeros_like(wk2)
    w2 = jnp.concatenate([jnp.concatenate([wk2, z2], 1), jnp.concatenate([z2, wv2], 1)], 0)
    return pos(0), pos(N_CHUNK_TOK), first_layer(0), first_layer(half), w2.astype(BF16)


def _flash_t(qst, k_scr, kt_scr, j_lo, j_hi, mask_fn, tq, tkv, s_scr, p_scr, acc_scr,
             static_chunks=None):
    G = NSA_GROUP
    last = j_hi - 1

    def scores(j, slot):
        jc = jnp.clip(j, j_lo, last)
        kv = k_scr[pl.ds(pl.multiple_of(jc * tkv, tkv), tkv), :]
        s_scr[slot] = _dot(kv, qst)

    def values(j, slot, a):
        jc = jnp.clip(j, j_lo, last)
        vt = kt_scr[jc, HEAD_DIM - SUBLANES:, :]
        acc_scr[...] = a * acc_scr[...] + _dot(vt, p_scr[slot])

    def softmax(j, slot, m):
        mask = mask_fn(j)
        ms = []
        for g in range(G):
            sl = slice(g * tq, (g + 1) * tq)
            sg = jnp.where(mask, s_scr[slot, :, sl], NEG)
            mg = jnp.maximum(m[:, sl], jnp.max(sg, axis=0, keepdims=True))
            p_scr[slot, :, sl] = jnp.exp2(sg - mg).astype(BF16)
            ms.append(mg)
        m_new = jnp.concatenate(ms, axis=1)
        return m_new, jnp.exp2(m - m_new)

    def result():
        acc = acc_scr[...]
        o = acc[SUBLANES:] / acc[SUBLANES - 1:SUBLANES]
        return [o[:, g * tq:(g + 1) * tq] for g in range(G)]

    m0 = jnp.full((1, G * tq), NEG, F32)
    acc_scr[...] = jnp.zeros_like(acc_scr)

    if static_chunks is not None:
        m = m0
        first = last - (static_chunks - 1)
        scores(first, 0)
        for c in range(static_chunks):
            if c + 1 < static_chunks:
                scores(first + c + 1, (c + 1) % 2)
            m, a = softmax(first + c, c % 2, m)
            values(first + c, c % 2, a)
        return result()

    p_scr[1] = jnp.zeros((tkv, G * tq), BF16)
    scores(j_lo, 0)

    def body(t, carry):
        m, a_prev = carry
        j0 = j_lo + 2 * t
        scores(j0 + 1, 1)
        values(j0 - 1, 1, a_prev)
        m, a0 = softmax(j0, 0, m)
        scores(j0 + 2, 0)
        values(j0, 0, a0)
        return softmax(j0 + 1, 1, m)

    trips = (j_hi - j_lo + 1) // 2
    _, a_last = lax.fori_loop(0, trips, body, (m0, jnp.ones((1, G * tq), F32)))
    values(j_lo + 2 * trips - 1, 1, a_last)
    return result()


def _nsa_attn_kernel(qt_ref, cos_ref, sin_ref, kc_ref, ks1_ref, ks2_ref, kvc_ref, kvs_ref,
                     kvw_ref, gt_ref, o_ref, ks_scr, kst_scr, kw_scr, kwt_scr, s_scr, p_scr,
                     acc_scr, *, tq, tkv):
    G = NSA_GROUP
    hk = pl.program_id(1)
    qi = pl.program_id(2)
    S = kvs_ref.shape[2]
    n_sel = S // SEL_BLOCK

    @pl.when(qi == 0)
    def _():
        is_key = lax.broadcasted_iota(jnp.int32, (S, LANES), 1) < HEAD_DIM
        c = jnp.where(is_key, kc_ref[0], 1.0)
        s1 = jnp.where(is_key, ks1_ref[0], 0.0)
        s2 = jnp.where(is_key, ks2_ref[0], 0.0)
        for src, scr, scr_t in ((kvs_ref, ks_scr, kst_scr), (kvw_ref, kw_scr, kwt_scr)):
            kr = _rope128(src[0, 0].astype(F32), c, s1, s2)
            scr[...] = kr.astype(BF16)
            ones_row = lax.broadcasted_iota(jnp.int32, (LANES, tkv), 0) == HEAD_DIM - 1
            for jj in range(S // tkv):
                krt = kr[jj * tkv:(jj + 1) * tkv].T
                scr_t[jj] = jnp.where(ones_row, 1.0, krt).astype(BF16)

    q = qt_ref[0].astype(F32)
    cos, sin = cos_ref[0], sin_ref[0]
    zpad = jnp.zeros((LANES - HEAD_DIM, tq), F32)
    plain, rot = [], []
    for g in range(G):
        x = q[g * HEAD_DIM:(g + 1) * HEAD_DIM]
        x1, x2 = x[:ROPE_HALF], x[ROPE_HALF:ROPE_DIM]
        xr = jnp.concatenate([x1 * cos - x2 * sin, x2 * cos + x1 * sin, x[ROPE_DIM:]], axis=0)
        plain.append(jnp.concatenate([x, zpad], axis=0))
        rot.append(jnp.concatenate([xr, zpad], axis=0))
    qst_plain = jnp.concatenate(plain, axis=1).astype(BF16)
    qst_rot = jnp.concatenate(rot, axis=1).astype(BF16)

    kvc = kvc_ref[0, 0]
    c_idx = lax.broadcasted_iota(jnp.int32, (LANES, tq), 0)
    t_pos = qi * tq + lax.broadcasted_iota(jnp.int32, (LANES, tq), 1)
    c_valid = (CMP_STRIDE * c_idx + CMP_BLOCK - 1) <= t_pos
    c_valid_f = jnp.where(c_valid, 1.0, 0.0)
    s = _dot(kvc.astype(BF16), qst_plain)
    ps = []
    for g in range(G):
        sg = jnp.where(c_valid, s[:, g * tq:(g + 1) * tq], NEG)
        e = jnp.exp2(sg - jnp.max(sg, axis=0, keepdims=True))
        ps.append(e / jnp.sum(e, axis=0, keepdims=True) * c_valid_f)
    o_cmp = _dot(kvc.T.astype(BF16), jnp.concatenate(ps, axis=1).astype(BF16))

    p_sum = ps[0]
    for g in range(1, G):
        p_sum = p_sum + ps[g]
    jb0 = lax.broadcasted_iota(jnp.int32, (LANES, LANES), 0) * SEL_BLOCK
    cb = lax.broadcasted_iota(jnp.int32, (LANES, LANES), 1) * CMP_STRIDE
    overlap = jnp.where((cb < jb0 + SEL_BLOCK) & (cb + CMP_BLOCK > jb0), 1.0, 0.0).astype(BF16)
    p_hi = p_sum.astype(BF16)
    p_lo = (p_sum - p_hi.astype(F32)).astype(BF16)
    imp = (_dot(overlap, p_hi) + _dot(overlap, p_lo))[:n_sel]
    jrow = lax.broadcasted_iota(jnp.int32, (n_sel, tq), 0)
    tb = jnp.right_shift(qi * tq + lax.broadcasted_iota(jnp.int32, (n_sel, tq), 1), SEL_SHIFT)
    forced = (jrow == 0) | (jrow == tb) | (jrow == tb - 1)
    score = jnp.where(forced, FORCE_SCORE, jnp.where(jrow <= tb, imp, -1.0))
    rank = jnp.zeros((n_sel, tq), F32)
    for j2 in range(n_sel):
        other = score[j2:j2 + 1, :]
        beats = (other > score) | ((other == score) & (jrow > j2))
        rank = rank + jnp.where(beats, 1.0, 0.0)
    chosen = jnp.where(rank < min(SEL_TOPN, n_sel), 1.0, 0.0).astype(BF16)

    key_r = lax.broadcasted_iota(jnp.int32, (tkv, tq), 0)
    qry = qi * tq + lax.broadcasted_iota(jnp.int32, (tkv, tq), 1)
    e_key = lax.broadcasted_iota(jnp.int32, (tkv, n_sel), 0)
    e_blk = lax.broadcasted_iota(jnp.int32, (tkv, n_sel), 1)

    def sel_mask(j):
        expand = jnp.where(jnp.right_shift(j * tkv + e_key, SEL_SHIFT) == e_blk, 1.0, 0.0)
        member = _dot(expand.astype(BF16), chosen)
        return (member > 0.5) & (j * tkv + key_r <= qry)

    def win_mask(j):
        key = j * tkv + key_r
        d = qry - key
        return (d >= 0) & (d <= WIN - 1) & (key >= 0)

    hi_chunk = (qi * tq + tq + tkv - 1) // tkv
    flash_scr = (s_scr, p_scr, acc_scr)
    o_sel = _flash_t(qst_rot, ks_scr, kst_scr, 0, hi_chunk, sel_mask, tq, tkv, *flash_scr)
    win_chunks = (tq + WIN - 1 + tkv - 1) // tkv
    o_win = _flash_t(qst_rot, kw_scr, kwt_scr, 0, hi_chunk, win_mask, tq, tkv, *flash_scr,
                     static_chunks=win_chunks)

    heads = []
    for g in range(G):
        sl = slice(g * tq, (g + 1) * tq)
        gate = lambda i: gt_ref[0, pl.ds(i * NSA_HEADS + hk * G + g, 1), :]
        heads.append(gate(0) * o_cmp[HEAD_DIM:, sl] + gate(1) * o_sel[g] + gate(2) * o_win[g])
    o_ref[0] = jnp.concatenate(heads, axis=0).T.astype(o_ref.dtype)


def _nsa_attention(qt, cos_t, sin_t, tabs, kvcmp, kvs, kvw, gt, tq=256, tkv=256):
    B, _, S = qt.shape
    HK = NSA_KV_HEADS
    gw = NSA_GROUP * HEAD_DIM
    tab_q = pl.BlockSpec((1, ROPE_HALF, tq), lambda b, h, i: (b, 0, i))
    tab_k = pl.BlockSpec((1, S, LANES), lambda b, h, i: (b, 0, 0))
    kv_spec = pl.BlockSpec((1, 1, S, LANES), lambda b, h, i: (b, h, 0, 0))
    kv_scr = pltpu.VMEM((S, LANES), BF16)
    kvt_scr = pltpu.VMEM((S // tkv, LANES, tkv), BF16)
    return pl.pallas_call(
        functools.partial(_nsa_attn_kernel, tq=tq, tkv=tkv),
        grid=(B, HK, S // tq),
        in_specs=[
            pl.BlockSpec((1, gw, tq), lambda b, h, i: (b, h, i)),
            tab_q, tab_q, tab_k, tab_k, tab_k,
            pl.BlockSpec((1, 1, kvcmp.shape[2], LANES), lambda b, h, i: (b, h, 0, 0)),
            kv_spec, kv_spec,
            pl.BlockSpec((1, LANES, tq), lambda b, h, i: (b, 0, i)),
        ],
        out_specs=pl.BlockSpec((1, tq, gw), lambda b, h, i: (b, i, h)),
        out_shape=jax.ShapeDtypeStruct((B, S, NSA_Q), BF16),
        scratch_shapes=[kv_scr, kvt_scr, kv_scr, kvt_scr,
                        pltpu.VMEM((2, tkv, NSA_GROUP * tq), F32),
                        pltpu.VMEM((2, tkv, NSA_GROUP * tq), BF16),
                        pltpu.VMEM((SUBLANES + HEAD_DIM, NSA_GROUP * tq), F32)],
        compiler_params=_cparams(("parallel", "parallel", "arbitrary")),
        name="nsa_attention",
    )(qt, cos_t, sin_t, *tabs, kvcmp, kvs, kvw, gt)


def _out_proj_kernel(o_ref, w_ref, g_ref, h_ref, y_ref):
    m = _dot(o_ref[0], w_ref[...])
    y_ref[0] = h_ref[0] + _rms(m, g_ref[...])


def _out_proj(o, w_o, g_post, h, tm=512):
    B, S, D = h.shape
    row = pl.BlockSpec((1, tm, D), lambda b, i: (b, i, 0))
    return pl.pallas_call(
        _out_proj_kernel,
        grid=(B, S // tm),
        in_specs=[row, pl.BlockSpec((D, D), lambda b, i: (0, 0)),
                  pl.BlockSpec((1, D), lambda b, i: (0, 0)), row],
        out_specs=row,
        out_shape=jax.ShapeDtypeStruct((B, S, D), F32),
        compiler_params=_cparams(("parallel", "parallel")),
        name="out_proj",
    )(o, w_o, g_post, h)


def _dil_out_kernel(o0_ref, o1_ref, o2_ref, l0_ref, l1_ref, l2_ref, w_ref, g_ref, h_ref, y_ref,
                    *scratch):
    for ref, scr in zip((o0_ref, o1_ref, o2_ref, l0_ref, l1_ref, l2_ref), scratch):
        dil, rows = ref.shape[1], ref.shape[2]
        for r in range(dil):
            rows_r = slice(None) if dil == 1 else pl.ds(r, rows, stride=dil)
            for ct in range(ref.shape[3] // LANES):
                scr[ct, rows_r, :] = ref[0, r, :, ct * LANES:(ct + 1) * LANES].astype(F32)
    outs, lses = scratch[:3], [s[0] for s in scratch[3:]]
    mx = jnp.maximum(jnp.maximum(lses[0], lses[1]), lses[2])
    ex = [jnp.exp(l - mx) for l in lses]
    den = ex[0] + ex[1] + ex[2]
    alphas = [e / den for e in ex]
    tm = h_ref.shape[1]
    lo = lax.broadcasted_iota(jnp.int32, (tm, LANES), 1) < HEAD_DIM
    slabs = []
    for c in range(DIL_HEADS // 2):
        acc = jnp.zeros((tm, LANES), F32)
        for n in range(3):
            a = jnp.where(lo, alphas[n][:, 2 * c:2 * c + 1], alphas[n][:, 2 * c + 1:2 * c + 2])
            acc = acc + a * outs[n][c]
        slabs.append(acc.astype(BF16))
    o = jnp.concatenate(slabs, axis=1)
    y_ref[0] = h_ref[0] + _rms(_dot(o, w_ref[...]), g_ref[...])


def _dil_out(outs, lses, w_o, g_post, h, tm=512):
    B, S, D = h.shape
    row = pl.BlockSpec((1, tm, D), lambda b, i: (b, i, 0))
    cls = lambda x: pl.BlockSpec((1, x.shape[1], tm // x.shape[1], x.shape[3]),
                                 lambda b, i: (b, 0, i, 0))
    return pl.pallas_call(
        _dil_out_kernel,
        grid=(B, S // tm),
        in_specs=[cls(x) for x in (*outs, *lses)] + [
            pl.BlockSpec((D, D), lambda b, i: (0, 0)),
            pl.BlockSpec((1, D), lambda b, i: (0, 0)), row],
        out_specs=row,
        out_shape=jax.ShapeDtypeStruct((B, S, D), F32),
        scratch_shapes=([pltpu.VMEM((D // LANES, tm, LANES), F32)] * 3
                        + [pltpu.VMEM((1, tm, LANES), F32)] * 3),
        compiler_params=_cparams(("parallel", "parallel")),
        name="dil_out_proj",
    )(*outs, *lses, w_o, g_post, h)


def _class_major(ref, dil):
    rows = ref.shape[0] // dil
    return jnp.concatenate([ref[pl.ds(r, rows, stride=dil), :] for r in range(dil)], axis=0)


def _dil_proj_kernel(h_ref, c_ref, s1_ref, s2_ref, g_ref, w_ref, q_ref, k_ref, v_ref, x_scr,
                     *, dil):
    D = D_MODEL
    rows = h_ref.shape[1] // dil
    if dil == 1:
        x = h_ref[0]
        c, s1, s2 = c_ref[0], s1_ref[0], s2_ref[0]
    else:
        for ct in range(D // LANES):
            x_scr[ct] = h_ref[0, :, ct * LANES:(ct + 1) * LANES]
        x = jnp.concatenate([_class_major(x_scr.at[ct], dil) for ct in range(D // LANES)],
                            axis=1)
        c, s1, s2 = (_class_major(ref.at[0], dil) for ref in (c_ref, s1_ref, s2_ref))
    a = _rms(x, g_ref[...]).astype(BF16)
    w = w_ref[...]
    q = _rope_wide(_dot(a, w[:, :D]), c, s1, s2) * (SCALE * LOG2E)
    k = _rope_wide(_dot(a, w[:, D:2 * D]), c, s1, s2)
    v = _dot(a, w[:, 2 * D:])
    for r in range(dil):
        sl = slice(r * rows, (r + 1) * rows)
        q_ref[0, r] = q[sl].astype(BF16)
        k_ref[0, r] = k[sl].astype(BF16)
        v_ref[0, r] = v[sl].astype(BF16)


def _dil_proj(h, tabs, g_pre, w, dil, tm=512):
    B, S, D = h.shape
    out_shape = jax.ShapeDtypeStruct((B, dil, S // dil, D), BF16)
    out_spec = pl.BlockSpec((1, dil, tm // dil, D), lambda b, i: (b, 0, i, 0))
    tab_spec = pl.BlockSpec((1, tm, LANES), lambda b, i: (b, i, 0))
    return pl.pallas_call(
        functools.partial(_dil_proj_kernel, dil=dil),
        grid=(B, S // tm),
        in_specs=[
            pl.BlockSpec((1, tm, D), lambda b, i: (b, i, 0)),
            tab_spec, tab_spec, tab_spec,
            pl.BlockSpec((1, D), lambda b, i: (0, 0)),
            pl.BlockSpec((D, 3 * D), lambda b, i: (0, 0)),
        ],
        out_specs=[out_spec, out_spec, out_spec],
        out_shape=[out_shape, out_shape, out_shape],
        scratch_shapes=[pltpu.VMEM((D // LANES, tm, LANES), F32)],
        compiler_params=_cparams(("parallel", "parallel")),
        name=f"dil_proj_d{dil}",
    )(h, *tabs, g_pre, w)


def _dil_attn_kernel(q_ref, kp_ref, kc_ref, vp_ref, vc_ref, o_ref, lse_ref, *, band, has_prev):
    tq = DIL_BAND_BLOCK
    i = pl.program_id(2)
    nk = 2 * tq if has_prev else tq
    key = lax.broadcasted_iota(jnp.int32, (nk, tq), 0)
    rel = lax.broadcasted_iota(jnp.int32, (nk, tq), 1) + (nk - tq) - key
    mask = (rel >= 0) & (rel <= band)
    if has_prev:
        mask = mask & ((key >= tq) | (i > 0))
    tr = lambda ref: ref[0, 0].astype(F32).T.astype(BF16)
    qt = tr(q_ref)
    if has_prev:
        k = jnp.concatenate([kp_ref[0, 0], kc_ref[0, 0]], axis=0)
        vt = jnp.concatenate([tr(vp_ref), tr(vc_ref)], axis=1)
    else:
        k = kc_ref[0, 0]
        vt = tr(vc_ref)
    zpad = jnp.zeros((HEAD_DIM, tq), BF16)
    scores = []
    for h in range(DIL_HEADS):
        qh = qt[h * HEAD_DIM:(h + 1) * HEAD_DIM]
        pair = jnp.concatenate([qh, zpad] if h % 2 == 0 else [zpad, qh], axis=0)
        scores.append(_dot(k[:, (h // 2) * LANES:(h // 2 + 1) * LANES], pair))
    probs, stats = [], []
    for s in scores:
        s = jnp.where(mask, s, NEG)
        m = jnp.max(s, axis=0, keepdims=True)
        p = jnp.exp2(s - m)
        probs.append(p.astype(BF16))
        stats.append((m, jnp.sum(p, axis=0, keepdims=True)))
    outs = [_dot(vt[h * HEAD_DIM:(h + 1) * HEAD_DIM], probs[h]) / stats[h][1]
            for h in range(DIL_HEADS)]
    o_ref[0, 0] = jnp.concatenate(outs, axis=0).T.astype(o_ref.dtype)
    lse = [m * LN2 + jnp.log(l) for m, l in stats]
    lse.append(jnp.zeros((LANES - DIL_HEADS, tq), F32))
    lse_ref[0, 0] = jnp.concatenate(lse, axis=0).T


def _dil_attention(q, k, v, band):
    B, dil, L, D = q.shape
    tq = DIL_BAND_BLOCK
    has_prev = L > tq
    cur = pl.BlockSpec((1, 1, tq, D), lambda b, r, i: (b, r, i, 0))
    prev = pl.BlockSpec((1, 1, tq, D), lambda b, r, i: (b, r, jnp.maximum(i - 1, 0), 0))
    return pl.pallas_call(
        functools.partial(_dil_attn_kernel, band=band, has_prev=has_prev),
        grid=(B, dil, L // tq),
        in_specs=[cur, prev, cur, prev, cur],
        out_specs=[cur, pl.BlockSpec((1, 1, tq, LANES), lambda b, r, i: (b, r, i, 0))],
        out_shape=[jax.ShapeDtypeStruct((B, dil, L, D), BF16),
                   jax.ShapeDtypeStruct((B, dil, L, LANES), F32)],
        compiler_params=_cparams(("parallel", "parallel", "arbitrary")),
        name=f"dil_attn_d{dil}",
    )(q, k, k, v, v)


def _ffn_kernel(h_ref, gpre_ref, wg_ref, wu_ref, wd_ref, gpost_ref, y_ref, a_scr, acc_scr):
    j = pl.program_id(2)

    @pl.when(j == 0)
    def _():
        a_scr[...] = _rms(h_ref[0], gpre_ref[...]).astype(BF16)
        acc_scr[...] = jnp.zeros_like(acc_scr)

    a = a_scr[...]
    g = _dot(a, wg_ref[...])
    u = _dot(a, wu_ref[...])
    mid = (g * _sigmoid(g) * u).astype(BF16)
    acc_scr[...] += _dot(mid, wd_ref[...])

    @pl.when(j == pl.num_programs(2) - 1)
    def _():
        y_ref[0] = h_ref[0] + _rms(acc_scr[...], gpost_ref[...])


def _ffn(h, g_pre, w_gu, w_down, g_post, tm=512, tf=1408):
    B, S, D = h.shape
    nf = D_FF // tf
    row = pl.BlockSpec((1, tm, D), lambda b, i, j: (b, i, 0))
    vec = pl.BlockSpec((1, D), lambda b, i, j: (0, 0))
    return pl.pallas_call(
        _ffn_kernel,
        grid=(B, S // tm, nf),
        in_specs=[row, vec,
                  pl.BlockSpec((D, tf), lambda b, i, j: (0, j)),
                  pl.BlockSpec((D, tf), lambda b, i, j: (0, nf + j)),
                  pl.BlockSpec((tf, D), lambda b, i, j: (j, 0)),
                  vec],
        out_specs=row,
        out_shape=jax.ShapeDtypeStruct((B, S, D), F32),
        scratch_shapes=[pltpu.VMEM((tm, D), BF16), pltpu.VMEM((tm, D), F32)],
        compiler_params=_cparams(("parallel", "parallel", "arbitrary")),
        name="swiglu_ffn",
    )(h, g_pre, w_gu, w_gu, w_down, g_post)


def _nsa_layer(h, rope, g_pre, g_post, w_in, cmp_pos, wk1, wk2, wv1, wv2, w_o):
    tabs, cos_t, sin_t = rope
    qt, kvc, kvs, kvw, gt = _nsa_proj(h, g_pre, *_nsa_in_weight(w_in))
    kvcmp = _compress(kvc, *_compress_weights(cmp_pos, wk1, wk2, wv1, wv2))
    o = _nsa_attention(qt, cos_t, sin_t, tabs, kvcmp, kvs, kvw, gt)
    return _out_proj(o, w_o.astype(BF16), g_post, h)


def _dil_layer(h, tabs, g_pre, g_post, w_in, w_o):
    D = D_MODEL
    outs, lses = [], []
    for gi, (window, dil) in enumerate(DIL_PATTERNS):
        w = w_in[:, gi * 3 * D:(gi + 1) * 3 * D].astype(BF16)
        q, k, v = _dil_proj(h, tabs, g_pre, w, dil)
        o, lse = _dil_attention(q, k, v, window // dil)
        outs.append(o)
        lses.append(lse)
    return _dil_out(outs, lses, w_o.astype(BF16), g_post, h)


def kernel(x, positions, norm_mix_pre, norm_mix_post, norm_ffn_pre, norm_ffn_post, ffn_w_gu,
           ffn_w_down, nsa_w_in, nsa_cmp_pos, nsa_cmp_wk1, nsa_cmp_wk2, nsa_cmp_wv1,
           nsa_cmp_wv2, nsa_w_o, dil_w_in, dil_w_o):
    depth = norm_mix_pre.shape[0]
    rope = _rope_tables(positions)
    tabs = rope[0]
    vec = lambda g: g.reshape(1, D_MODEL)
    h = x
    for i in range(depth):
        j = i // 2
        if i % 2 == 0:
            h = _nsa_layer(h, rope, vec(norm_mix_pre[i]), vec(norm_mix_post[i]), nsa_w_in[j],
                           nsa_cmp_pos[j], nsa_cmp_wk1[j], nsa_cmp_wk2[j], nsa_cmp_wv1[j],
                           nsa_cmp_wv2[j], nsa_w_o[j])
        else:
            h = _dil_layer(h, tabs, vec(norm_mix_pre[i]), vec(norm_mix_post[i]),
                           dil_w_in[j], dil_w_o[j])
        h = _ffn(h, vec(norm_ffn_pre[i]), ffn_w_gu[i].astype(BF16), ffn_w_down[i].astype(BF16),
                 vec(norm_ffn_post[i]))
    return h
```

```python
import functools

import jax
import jax.numpy as jnp
from jax import lax
from jax.experimental import pallas as pl
from jax.experimental.pallas import tpu as pltpu

F32 = jnp.float32
BF16 = jnp.bfloat16

D_MODEL = 1024
HEAD_DIM = 64
HEAD_SHIFT = 6
ROPE_DIM = HEAD_DIM // 4
ROPE_HALF = ROPE_DIM // 2
ROPE_THETA = 500000.0
NORM_EPS = 1e-6
NEG = -1e30
SCALE = HEAD_DIM ** -0.5
LOG2E = 1.4426950408889634
LN2 = 0.6931471805599453

NSA_HEADS = 16
NSA_KV_HEADS = 4
NSA_GROUP = NSA_HEADS // NSA_KV_HEADS
CMP_BLOCK = 32
CMP_STRIDE = 16
CMP_HIDDEN = 4 * HEAD_DIM
SEL_BLOCK = 64
SEL_SHIFT = 6
SEL_TOPN = 8
FORCE_SCORE = 1e4
WIN = 512
NSA_Q = NSA_HEADS * HEAD_DIM
NSA_KV = NSA_KV_HEADS * 2 * HEAD_DIM
NSA_GATES = 3 * NSA_HEADS
NSA_W_COLS = NSA_Q + 3 * NSA_KV + 128

DIL_PATTERNS = ((128, 1), (512, 4), (2048, 16))
DIL_HEADS = 16
DIL_BAND_BLOCK = 128

D_FF = 2816

LANES = 128
SUBLANES = 8
VMEM_LIMIT = 56 * 1024 * 1024


def _cparams(sem):
    return pltpu.CompilerParams(dimension_semantics=sem, vmem_limit_bytes=VMEM_LIMIT)


def _rms(x, g):
    ms = jnp.mean(x * x, axis=-1, keepdims=True)
    return x * lax.rsqrt(ms + NORM_EPS) * g


def _sigmoid(x):
    return 1.0 / (1.0 + jnp.exp(-x))


def _rope128(x, c, s1, s2):
    return x * c + pltpu.roll(x, ROPE_HALF, 1) * s1 + pltpu.roll(x, LANES - ROPE_HALF, 1) * s2


def _rope_wide(x, c, s1, s2):
    n = x.shape[1] // LANES
    return jnp.concatenate(
        [_rope128(x[:, i * LANES:(i + 1) * LANES], c, s1, s2) for i in range(n)], axis=1)


def _dot(a, b):
    return jnp.dot(a, b, preferred_element_type=F32)


def _dot_t(a, b):
    return lax.dot_general(a, b, (((1,), (1,)), ((), ())), preferred_element_type=F32)


def _rope_tables(positions):
    inv = ROPE_THETA ** (-jnp.arange(0, ROPE_DIM, 2, dtype=F32) / ROPE_DIM)
    ang = positions.astype(F32)[..., None] * inv
    cos, sin = jnp.cos(ang), jnp.sin(ang)
    one = jnp.ones(ang.shape[:-1] + (HEAD_DIM - ROPE_DIM,), F32)
    zero = jnp.zeros_like(one)
    z8 = jnp.zeros_like(sin)
    c64 = jnp.concatenate([cos, cos, one], axis=-1)
    s1 = jnp.concatenate([z8, sin, zero], axis=-1)
    s2 = jnp.concatenate([-sin, z8, zero], axis=-1)
    tile = lambda t: jnp.concatenate([t, t], axis=-1)
    lane_tabs = (tile(c64), tile(s1), tile(s2))
    return lane_tabs, cos.transpose(0, 2, 1), sin.transpose(0, 2, 1)


def _nsa_proj_kernel(h_ref, g_ref, wt_ref, wkv_ref, qt_ref, kvc_ref, kvs_ref, kvw_ref, gt_ref):
    a = _rms(h_ref[0], g_ref[...]).astype(BF16)
    rt = _dot_t(wt_ref[...], a)
    qt_ref[0] = (rt[:NSA_Q] * (SCALE * LOG2E)).astype(BF16)
    gt_ref[0] = _sigmoid(rt[NSA_Q:])
    r = _dot(a, wkv_ref[...])
    for i, ref in enumerate((kvc_ref, kvs_ref, kvw_ref)):
        for hk in range(NSA_KV_HEADS):
            lo = i * NSA_KV + hk * LANES
            ref[0, hk] = r[:, lo:lo + LANES].astype(BF16)


def _nsa_proj(h, g_pre, wt, wkv, tm=512):
    B, S, D = h.shape
    kv_shape = jax.ShapeDtypeStruct((B, NSA_KV_HEADS, S, LANES), BF16)
    kv_spec = pl.BlockSpec((1, NSA_KV_HEADS, tm, LANES), lambda b, i: (b, 0, i, 0))
    return pl.pallas_call(
        _nsa_proj_kernel,
        grid=(B, S // tm),
        in_specs=[
            pl.BlockSpec((1, tm, D), lambda b, i: (b, i, 0)),
            pl.BlockSpec((1, D), lambda b, i: (0, 0)),
            pl.BlockSpec((NSA_Q + LANES, D), lambda b, i: (0, 0)),
            pl.BlockSpec((D, 3 * NSA_KV), lambda b, i: (0, 0)),
        ],
        out_specs=[
            pl.BlockSpec((1, NSA_Q, tm), lambda b, i: (b, 0, i)),
            kv_spec, kv_spec, kv_spec,
            pl.BlockSpec((1, LANES, tm), lambda b, i: (b, 0, i)),
        ],
        out_shape=[
            jax.ShapeDtypeStruct((B, NSA_Q, S), BF16),
            kv_shape, kv_shape, kv_shape,
            jax.ShapeDtypeStruct((B, LANES, S), F32),
        ],
        compiler_params=_cparams(("parallel", "parallel")),
        name="nsa_proj",
    )(h, g_pre, wt, wkv)


def _nsa_in_weight(w_in):
    dq, dkv = NSA_Q, NSA_KV_HEADS * HEAD_DIM
    kv = w_in[:, dq:dq + 6 * dkv].reshape(D_MODEL, 3, 2, NSA_KV_HEADS, HEAD_DIM)
    kv = kv.transpose(0, 1, 3, 2, 4).reshape(D_MODEL, 3 * NSA_KV)
    gates = jnp.pad(w_in[:, dq + 6 * dkv:], ((0, 0), (0, LANES - NSA_GATES)))
    wt = jnp.concatenate([w_in[:, :dq], gates], axis=1).T
    return wt.astype(BF16), kv.astype(BF16)


N_CHUNK_TOK = CMP_STRIDE
CHUNK_COLS = N_CHUNK_TOK * 2 * HEAD_DIM


def _compress_kernel(x_ref, pa_ref, pb_ref, wa_ref, wb_ref, w2_ref, o_ref):
    x = x_ref[0, 0].astype(F32)
    ha = _dot((x + pa_ref[...]).astype(BF16), wa_ref[...])
    hb = _dot((x + pb_ref[...]).astype(BF16), wb_ref[...])
    n = x.shape[0]
    hid = ha + pltpu.roll(hb, n - 1, 0)
    act = hid * _sigmoid(hid)
    o_ref[0, 0] = _dot(act.astype(BF16), w2_ref[...])


def _compress(kvc, pa, pb, wa, wb, w2):
    B, HK, S, _ = kvc.shape
    nchunk = S // N_CHUNK_TOK
    x = kvc.reshape(B, HK, nchunk, CHUNK_COLS)
    full = lambda shape: pl.BlockSpec(shape, lambda b, h: (0,) * len(shape))
    return pl.pallas_call(
        _compress_kernel,
        grid=(B, HK),
        in_specs=[
            pl.BlockSpec((1, 1, nchunk, CHUNK_COLS), lambda b, h: (b, h, 0, 0)),
            full((1, CHUNK_COLS)), full((1, CHUNK_COLS)),
            full((CHUNK_COLS, 2 * CMP_HIDDEN)), full((CHUNK_COLS, 2 * CMP_HIDDEN)),
            full((2 * CMP_HIDDEN, LANES)),
        ],
        out_specs=pl.BlockSpec((1, 1, nchunk, LANES), lambda b, h: (b, h, 0, 0)),
        out_shape=jax.ShapeDtypeStruct((B, HK, nchunk, LANES), F32),
        compiler_params=_cparams(("parallel", "parallel")),
        name="nsa_compress",
    )(x, pa, pb, wa, wb, w2)


def _compress_weights(cmp_pos, wk1, wk2, wv1, wv2):
    half = N_CHUNK_TOK * HEAD_DIM

    def first_layer(lo):
        wk = wk1[lo:lo + half].reshape(N_CHUNK_TOK, HEAD_DIM, CMP_HIDDEN)
        wv = wv1[lo:lo + half].reshape(N_CHUNK_TOK, HEAD_DIM, CMP_HIDDEN)
        z = jnp.zeros_like(wk)
        w = jnp.stack([jnp.concatenate([wk, z], -1), jnp.concatenate([z, wv], -1)], axis=1)
        return w.reshape(CHUNK_COLS, 2 * CMP_HIDDEN).astype(BF16)

    def pos(lo):
        p = cmp_pos[lo:lo + N_CHUNK_TOK]
        return jnp.stack([p, p], axis=1).reshape(1, CHUNK_COLS)

    z2 = jnp.zeros_like(wk2)
    w2 = jnp.concatenate([jnp.concatenate([wk2, z2], 1), jnp.concatenate([z2, wv2], 1)], 0)
    return pos(0), pos(N_CHUNK_TOK), first_layer(0), first_layer(half), w2.astype(BF16)


class _Flash:
    def __init__(self, qst, k_scr, kt_scr, tq, tkv, s_scr, p_scr, acc_scr):
        self.qst, self.k_scr, self.kt_scr = qst, k_scr, kt_scr
        self.tq, self.tkv = tq, tkv
        self.s_scr, self.p_scr, self.acc_scr = s_scr, p_scr, acc_scr
        self.last = kt_scr.shape[0] - 1
        width = NSA_GROUP * tq
        self.m0 = jnp.full((1, width), NEG, F32)
        self.a0 = jnp.ones((1, width), F32)
        acc_scr[...] = jnp.zeros_like(acc_scr)

    def scores(self, j, slot):
        off = pl.multiple_of(jnp.clip(j, 0, self.last) * self.tkv, self.tkv)
        self.s_scr[slot] = _dot(self.k_scr[pl.ds(off, self.tkv), :], self.qst)

    def values(self, j, slot, a):
        vt = self.kt_scr[jnp.clip(j, 0, self.last), HEAD_DIM - SUBLANES:, :]
        self.acc_scr[...] = a * self.acc_scr[...] + _dot(vt, self.p_scr[slot])

    def softmax(self, slot, m, bias=None):
        tq, ms = self.tq, []
        for g in range(NSA_GROUP):
            sl = slice(g * tq, (g + 1) * tq)
            sg = self.s_scr[slot, :, sl]
            if bias is not None:
                sg = sg + bias
            mg = jnp.maximum(m[:, sl], jnp.max(sg, axis=0, keepdims=True))
            self.p_scr[slot, :, sl] = jnp.exp2(sg - mg).astype(BF16)
            ms.append(mg)
        m_new = jnp.concatenate(ms, axis=1)
        return m_new, jnp.exp2(m - m_new)

    def result(self):
        acc = self.acc_scr[...]
        o = acc[SUBLANES:] / acc[SUBLANES - 1:SUBLANES]
        return [o[:, g * self.tq:(g + 1) * self.tq] for g in range(NSA_GROUP)]


def _nsa_attn_kernel(qt_ref, cos_ref, sin_ref, kc_ref, ks1_ref, ks2_ref, kvc_ref, kvs_ref,
                     kvw_ref, gt_ref, o_ref, ks_scr, kst_scr, kw_scr, kwt_scr, s_scr, p_scr,
                     acc_scr, *, tq, tkv):
    G = NSA_GROUP
    hk = pl.program_id(1)
    qi = pl.program_id(2)
    S = kvs_ref.shape[2]
    n_sel = S // SEL_BLOCK

    @pl.when(qi == 0)
    def _():
        is_key = lax.broadcasted_iota(jnp.int32, (S, LANES), 1) < HEAD_DIM
        c = jnp.where(is_key, kc_ref[0], 1.0)
        s1 = jnp.where(is_key, ks1_ref[0], 0.0)
        s2 = jnp.where(is_key, ks2_ref[0], 0.0)
        lane = lax.broadcasted_iota(jnp.int32, (S, LANES), 1)
        blk = jnp.right_shift(lax.broadcasted_iota(jnp.int32, (S, LANES), 0), SEL_SHIFT)
        onehot = jnp.where(lane - HEAD_DIM == blk, 1.0, 0.0)
        for src, scr, scr_t in ((kvs_ref, ks_scr, kst_scr), (kvw_ref, kw_scr, kwt_scr)):
            kr = _rope128(src[0, 0].astype(F32), c, s1, s2)
            scr[...] = (jnp.where(is_key, kr, onehot) if scr is ks_scr else kr).astype(BF16)
            ones_row = lax.broadcasted_iota(jnp.int32, (LANES, tkv), 0) == HEAD_DIM - 1
            for jj in range(S // tkv):
                krt = kr[jj * tkv:(jj + 1) * tkv].T
                scr_t[jj] = jnp.where(ones_row, 1.0, krt).astype(BF16)

    q = qt_ref[0].astype(F32)
    cos, sin = cos_ref[0], sin_ref[0]
    zpad = jnp.zeros((LANES - HEAD_DIM, tq), F32)
    plain, rot = [], []
    for g in range(G):
        x = q[g * HEAD_DIM:(g + 1) * HEAD_DIM]
        x1, x2 = x[:ROPE_HALF], x[ROPE_HALF:ROPE_DIM]
        xr = jnp.concatenate([x1 * cos - x2 * sin, x2 * cos + x1 * sin, x[ROPE_DIM:]], axis=0)
        plain.append(jnp.concatenate([x, zpad], axis=0))
        rot.append(xr)
    qst_plain = jnp.concatenate(plain, axis=1).astype(BF16)
    qst_win = jnp.concatenate([jnp.concatenate([xr, zpad], axis=0) for xr in rot],
                              axis=1).astype(BF16)

    kvc = kvc_ref[0, 0]
    c_idx = lax.broadcasted_iota(jnp.int32, (LANES, tq), 0)
    t_pos = qi * tq + lax.broadcasted_iota(jnp.int32, (LANES, tq), 1)
    c_valid = (CMP_STRIDE * c_idx + CMP_BLOCK - 1) <= t_pos
    c_valid_f = jnp.where(c_valid, 1.0, 0.0)
    s = _dot(kvc.astype(BF16), qst_plain)
    ps = []
    for g in range(G):
        sg = jnp.where(c_valid, s[:, g * tq:(g + 1) * tq], NEG)
        e = jnp.exp2(sg - jnp.max(sg, axis=0, keepdims=True))
        ps.append(e / jnp.sum(e, axis=0, keepdims=True) * c_valid_f)
    o_cmp = _dot(kvc.T.astype(BF16), jnp.concatenate(ps, axis=1).astype(BF16))

    p_sum = ps[0]
    for g in range(1, G):
        p_sum = p_sum + ps[g]
    jb0 = lax.broadcasted_iota(jnp.int32, (LANES, LANES), 0) * SEL_BLOCK
    cb = lax.broadcasted_iota(jnp.int32, (LANES, LANES), 1) * CMP_STRIDE
    overlap = jnp.where((cb < jb0 + SEL_BLOCK) & (cb + CMP_BLOCK > jb0), 1.0, 0.0).astype(BF16)
    p_hi = p_sum.astype(BF16)
    p_lo = (p_sum - p_hi.astype(F32)).astype(BF16)
    imp = (_dot(overlap, p_hi) + _dot(overlap, p_lo))[:n_sel]
    jrow = lax.broadcasted_iota(jnp.int32, (n_sel, tq), 0)
    tb = jnp.right_shift(qi * tq + lax.broadcasted_iota(jnp.int32, (n_sel, tq), 1), SEL_SHIFT)
    forced = (jrow == 0) | (jrow == tb) | (jrow == tb - 1)
    score = jnp.where(forced, FORCE_SCORE, jnp.where(jrow <= tb, imp, -1.0))
    rank = jnp.zeros((n_sel, tq), F32)
    for j2 in range(n_sel):
        other = score[j2:j2 + 1, :]
        beats = (other > score) | ((other == score) & (jrow > j2))
        rank = rank + jnp.where(beats, 1.0, 0.0)
    sel_bias = jnp.where(rank < min(SEL_TOPN, n_sel), 0.0, NEG)
    zrest = jnp.zeros((LANES - HEAD_DIM - n_sel, tq), F32)
    qst_sel = jnp.concatenate([jnp.concatenate([xr, sel_bias, zrest], axis=0) for xr in rot],
                              axis=1).astype(BF16)

    key_r = lax.broadcasted_iota(jnp.int32, (tkv, tq), 0)
    qry = qi * tq + lax.broadcasted_iota(jnp.int32, (tkv, tq), 1)

    def causal_bias(j):
        return jnp.where(j * tkv + key_r <= qry, 0.0, NEG)

    def band_bias(j):
        d = qry - (j * tkv + key_r)
        inside = jnp.where(j >= 0, 0.0, NEG)
        return jnp.where(d >= 0, jnp.where(d <= WIN - 1, inside, NEG), NEG)

    flash_scr = (tq, tkv, s_scr, p_scr, acc_scr)
    assert tq == tkv

    fl = _Flash(qst_sel, ks_scr, kst_scr, *flash_scr)
    p_scr[1] = jnp.zeros((tkv, G * tq), BF16)
    fl.scores(0, 0)

    def pair(t, carry):
        m, a_prev = carry
        j0 = 2 * t
        fl.scores(j0 + 1, 1)
        fl.values(j0 - 1, 1, a_prev)
        m, a0 = fl.softmax(0, m)
        fl.scores(j0 + 2, 0)
        fl.values(j0, 0, a0)
        return fl.softmax(1, m)

    m, a_prev = lax.fori_loop(0, qi // 2, pair, (fl.m0, fl.a0))
    j0 = 2 * (qi // 2)
    fl.scores(j0 + 1, 1)
    fl.values(j0 - 1, 1, a_prev)
    m, a0 = fl.softmax(0, m, causal_bias(j0))
    fl.values(j0, 0, a0)
    m, a1 = fl.softmax(1, m, causal_bias(j0 + 1))
    fl.values(j0 + 1, 1, a1)
    o_sel = fl.result()

    fl = _Flash(qst_win, kw_scr, kwt_scr, *flash_scr)
    win_chunks = (tq + WIN - 1 + tkv - 1) // tkv
    first = qi - (win_chunks - 1)
    m = fl.m0
    fl.scores(first, 0)
    for c in range(win_chunks):
        if c + 1 < win_chunks:
            fl.scores(first + c + 1, (c + 1) % 2)
        m, a = fl.softmax(c % 2, m, band_bias(first + c))
        fl.values(first + c, c % 2, a)
    o_win = fl.result()

    heads = []
    for g in range(G):
        sl = slice(g * tq, (g + 1) * tq)
        gate = lambda i: gt_ref[0, pl.ds(i * NSA_HEADS + hk * G + g, 1), :]
        heads.append(gate(0) * o_cmp[HEAD_DIM:, sl] + gate(1) * o_sel[g] + gate(2) * o_win[g])
    o_ref[0] = jnp.concatenate(heads, axis=0).T.astype(o_ref.dtype)


def _nsa_attention(qt, cos_t, sin_t, tabs, kvcmp, kvs, kvw, gt, tq=256, tkv=256):
    B, _, S = qt.shape
    HK = NSA_KV_HEADS
    gw = NSA_GROUP * HEAD_DIM
    tab_q = pl.BlockSpec((1, ROPE_HALF, tq), lambda b, h, i: (b, 0, i))
    tab_k = pl.BlockSpec((1, S, LANES), lambda b, h, i: (b, 0, 0))
    kv_spec = pl.BlockSpec((1, 1, S, LANES), lambda b, h, i: (b, h, 0, 0))
    kv_scr = pltpu.VMEM((S, LANES), BF16)
    kvt_scr = pltpu.VMEM((S // tkv, LANES, tkv), BF16)
    return pl.pallas_call(
        functools.partial(_nsa_attn_kernel, tq=tq, tkv=tkv),
        grid=(B, HK, S // tq),
        in_specs=[
            pl.BlockSpec((1, gw, tq), lambda b, h, i: (b, h, i)),
            tab_q, tab_q, tab_k, tab_k, tab_k,
            pl.BlockSpec((1, 1, kvcmp.shape[2], LANES), lambda b, h, i: (b, h, 0, 0)),
            kv_spec, kv_spec,
            pl.BlockSpec((1, LANES, tq), lambda b, h, i: (b, 0, i)),
        ],
        out_specs=pl.BlockSpec((1, tq, gw), lambda b, h, i: (b, i, h)),
        out_shape=jax.ShapeDtypeStruct((B, S, NSA_Q), BF16),
        scratch_shapes=[kv_scr, kvt_scr, kv_scr, kvt_scr,
                        pltpu.VMEM((2, tkv, NSA_GROUP * tq), F32),
                        pltpu.VMEM((2, tkv, NSA_GROUP * tq), BF16),
                        pltpu.VMEM((SUBLANES + HEAD_DIM, NSA_GROUP * tq), F32)],
        compiler_params=_cparams(("parallel", "parallel", "arbitrary")),
        name="nsa_attention",
    )(qt, cos_t, sin_t, *tabs, kvcmp, kvs, kvw, gt)


def _out_proj_kernel(o_ref, w_ref, g_ref, h_ref, y_ref):
    m = _dot(o_ref[0], w_ref[...])
    y_ref[0] = h_ref[0] + _rms(m, g_ref[...])


def _out_proj(o, w_o, g_post, h, tm=512):
    B, S, D = h.shape
    row = pl.BlockSpec((1, tm, D), lambda b, i: (b, i, 0))
    return pl.pallas_call(
        _out_proj_kernel,
        grid=(B, S // tm),
        in_specs=[row, pl.BlockSpec((D, D), lambda b, i: (0, 0)),
                  pl.BlockSpec((1, D), lambda b, i: (0, 0)), row],
        out_specs=row,
        out_shape=jax.ShapeDtypeStruct((B, S, D), F32),
        compiler_params=_cparams(("parallel", "parallel")),
        name="out_proj",
    )(o, w_o, g_post, h)


def _dil_out_kernel(o0_ref, o1_ref, o2_ref, l0_ref, l1_ref, l2_ref, w_ref, g_ref, h_ref, y_ref,
                    *scratch):
    for ref, scr in zip((o0_ref, o1_ref, o2_ref, l0_ref, l1_ref, l2_ref), scratch):
        dil, rows = ref.shape[1], ref.shape[2]
        for r in range(dil):
            rows_r = slice(None) if dil == 1 else pl.ds(r, rows, stride=dil)
            for ct in range(ref.shape[3] // LANES):
                scr[ct, rows_r, :] = ref[0, r, :, ct * LANES:(ct + 1) * LANES].astype(F32)
    outs, lses = scratch[:3], [s[0] for s in scratch[3:]]
    mx = jnp.maximum(jnp.maximum(lses[0], lses[1]), lses[2])
    ex = [jnp.exp(l - mx) for l in lses]
    den = ex[0] + ex[1] + ex[2]
    alphas = [e / den for e in ex]
    D = h_ref.shape[2]
    col_head = jnp.right_shift(lax.broadcasted_iota(jnp.int32, (LANES, D), 1), HEAD_SHIFT)
    spread = jnp.where(col_head == lax.broadcasted_iota(jnp.int32, (LANES, D), 0),
                       1.0, 0.0).astype(BF16)
    wide = []
    for a in alphas:
        hi = a.astype(BF16)
        wide.append(_dot(hi, spread) + _dot((a - hi.astype(F32)).astype(BF16), spread))
    slabs = []
    for c in range(D // LANES):
        cols = slice(c * LANES, (c + 1) * LANES)
        acc = wide[0][:, cols] * outs[0][c]
        for n in range(1, 3):
            acc = acc + wide[n][:, cols] * outs[n][c]
        slabs.append(acc.astype(BF16))
    o = jnp.concatenate(slabs, axis=1)
    y_ref[0] = h_ref[0] + _rms(_dot(o, w_ref[...]), g_ref[...])


def _dil_out(outs, lses, w_o, g_post, h, tm=512):
    B, S, D = h.shape
    row = pl.BlockSpec((1, tm, D), lambda b, i: (b, i, 0))
    cls = lambda x: pl.BlockSpec((1, x.shape[1], tm // x.shape[1], x.shape[3]),
                                 lambda b, i: (b, 0, i, 0))
    return pl.pallas_call(
        _dil_out_kernel,
        grid=(B, S // tm),
        in_specs=[cls(x) for x in (*outs, *lses)] + [
            pl.BlockSpec((D, D), lambda b, i: (0, 0)),
            pl.BlockSpec((1, D), lambda b, i: (0, 0)), row],
        out_specs=row,
        out_shape=jax.ShapeDtypeStruct((B, S, D), F32),
        scratch_shapes=([pltpu.VMEM((D // LANES, tm, LANES), F32)] * 3
                        + [pltpu.VMEM((1, tm, LANES), F32)] * 3),
        compiler_params=_cparams(("parallel", "parallel")),
        name="dil_out_proj",
    )(*outs, *lses, w_o, g_post, h)


def _class_major(ref, dil):
    rows = ref.shape[0] // dil
    return jnp.concatenate([ref[pl.ds(r, rows, stride=dil), :] for r in range(dil)], axis=0)


def _dil_proj_kernel(h_ref, c_ref, s1_ref, s2_ref, g_ref, w_ref, q_ref, k_ref, v_ref, x_scr,
                     *, dil):
    D = D_MODEL
    rows = h_ref.shape[1] // dil
    if dil == 1:
        x = h_ref[0]
        c, s1, s2 = c_ref[0], s1_ref[0], s2_ref[0]
    else:
        for ct in range(D // LANES):
            x_scr[ct] = h_ref[0, :, ct * LANES:(ct + 1) * LANES]
        x = jnp.concatenate([_class_major(x_scr.at[ct], dil) for ct in range(D // LANES)],
                            axis=1)
        c, s1, s2 = (_class_major(ref.at[0], dil) for ref in (c_ref, s1_ref, s2_ref))
    a = _rms(x, g_ref[...]).astype(BF16)
    w = w_ref[...]
    q = _rope_wide(_dot(a, w[:, :D]), c, s1, s2) * (SCALE * LOG2E)
    k = _rope_wide(_dot(a, w[:, D:2 * D]), c, s1, s2)
    v = _dot(a, w[:, 2 * D:])
    for r in range(dil):
        sl = slice(r * rows, (r + 1) * rows)
        q_ref[0, r] = q[sl].astype(BF16)
        k_ref[0, r] = k[sl].astype(BF16)
        v_ref[0, r] = v[sl].astype(BF16)


def _dil_proj(h, tabs, g_pre, w, dil, tm=512):
    B, S, D = h.shape
    out_shape = jax.ShapeDtypeStruct((B, dil, S // dil, D), BF16)
    out_spec = pl.BlockSpec((1, dil, tm // dil, D), lambda b, i: (b, 0, i, 0))
    tab_spec = pl.BlockSpec((1, tm, LANES), lambda b, i: (b, i, 0))
    return pl.pallas_call(
        functools.partial(_dil_proj_kernel, dil=dil),
        grid=(B, S // tm),
        in_specs=[
            pl.BlockSpec((1, tm, D), lambda b, i: (b, i, 0)),
            tab_spec, tab_spec, tab_spec,
            pl.BlockSpec((1, D), lambda b, i: (0, 0)),
            pl.BlockSpec((D, 3 * D), lambda b, i: (0, 0)),
        ],
        out_specs=[out_spec, out_spec, out_spec],
        out_shape=[out_shape, out_shape, out_shape],
        scratch_shapes=[pltpu.VMEM((D // LANES, tm, LANES), F32)],
        compiler_params=_cparams(("parallel", "parallel")),
        name=f"dil_proj_d{dil}",
    )(h, *tabs, g_pre, w)


def _dil_attn_kernel(q_ref, kp_ref, kc_ref, vp_ref, vc_ref, o_ref, lse_ref, *, band, has_prev):
    tq = DIL_BAND_BLOCK
    i = pl.program_id(2)
    nk = 2 * tq if has_prev else tq
    key = lax.broadcasted_iota(jnp.int32, (nk, tq), 0)
    rel = lax.broadcasted_iota(jnp.int32, (nk, tq), 1) + (nk - tq) - key
    mask = (rel >= 0) & (rel <= band)
    if has_prev:
        mask = mask & ((key >= tq) | (i > 0))
    tr = lambda ref: ref[0, 0].astype(F32).T.astype(BF16)
    qt = tr(q_ref)
    if has_prev:
        k = jnp.concatenate([kp_ref[0, 0], kc_ref[0, 0]], axis=0)
        vt = jnp.concatenate([tr(vp_ref), tr(vc_ref)], axis=1)
    else:
        k = kc_ref[0, 0]
        vt = tr(vc_ref)
    zpad = jnp.zeros((HEAD_DIM, tq), BF16)
    scores = []
    for h in range(DIL_HEADS):
        qh = qt[h * HEAD_DIM:(h + 1) * HEAD_DIM]
        pair = jnp.concatenate([qh, zpad] if h % 2 == 0 else [zpad, qh], axis=0)
        scores.append(_dot(k[:, (h // 2) * LANES:(h // 2 + 1) * LANES], pair))
    probs, stats = [], []
    for s in scores:
        s = jnp.where(mask, s, NEG)
        m = jnp.max(s, axis=0, keepdims=True)
        p = jnp.exp2(s - m)
        probs.append(p.astype(BF16))
        stats.append((m, jnp.sum(p, axis=0, keepdims=True)))
    outs = [_dot(vt[h * HEAD_DIM:(h + 1) * HEAD_DIM], probs[h]) / stats[h][1]
            for h in range(DIL_HEADS)]
    o_ref[0, 0] = jnp.concatenate(outs, axis=0).T.astype(o_ref.dtype)
    lse = [m * LN2 + jnp.log(l) for m, l in stats]
    lse.append(jnp.zeros((LANES - DIL_HEADS, tq), F32))
    lse_ref[0, 0] = jnp.concatenate(lse, axis=0).T


def _dil_attention(q, k, v, band):
    B, dil, L, D = q.shape
    tq = DIL_BAND_BLOCK
    has_prev = L > tq
    cur = pl.BlockSpec((1, 1, tq, D), lambda b, r, i: (b, r, i, 0))
    prev = pl.BlockSpec((1, 1, tq, D), lambda b, r, i: (b, r, jnp.maximum(i - 1, 0), 0))
    return pl.pallas_call(
        functools.partial(_dil_attn_kernel, band=band, has_prev=has_prev),
        grid=(B, dil, L // tq),
        in_specs=[cur, prev, cur, prev, cur],
        out_specs=[cur, pl.BlockSpec((1, 1, tq, LANES), lambda b, r, i: (b, r, i, 0))],
        out_shape=[jax.ShapeDtypeStruct((B, dil, L, D), BF16),
                   jax.ShapeDtypeStruct((B, dil, L, LANES), F32)],
        compiler_params=_cparams(("parallel", "parallel", "arbitrary")),
        name=f"dil_attn_d{dil}",
    )(q, k, k, v, v)


FFN_CHUNK = 256


def _ffn_kernel(h_ref, gpre_ref, wgu_ref, wd_ref, gpost_ref, y_ref):
    a = _rms(h_ref[0], gpre_ref[...]).astype(BF16)
    acc = None
    for j in range(D_FF // FFN_CHUNK):
        cols = slice(j * FFN_CHUNK, (j + 1) * FFN_CHUNK)
        ucols = slice(D_FF + j * FFN_CHUNK, D_FF + (j + 1) * FFN_CHUNK)
        g = _dot(a, wgu_ref[:, cols])
        u = _dot(a, wgu_ref[:, ucols])
        mid = (g * _sigmoid(g) * u).astype(BF16)
        part = _dot(mid, wd_ref[cols, :])
        acc = part if acc is None else acc + part
    y_ref[0] = h_ref[0] + _rms(acc, gpost_ref[...])


def _ffn(h, g_pre, w_gu, w_down, g_post, tm=512):
    B, S, D = h.shape
    row = pl.BlockSpec((1, tm, D), lambda b, i: (b, i, 0))
    vec = pl.BlockSpec((1, D), lambda b, i: (0, 0))
    once = pl.Buffered(1)
    return pl.pallas_call(
        _ffn_kernel,
        grid=(B, S // tm),
        in_specs=[row, vec,
                  pl.BlockSpec((D, 2 * D_FF), lambda b, i: (0, 0), pipeline_mode=once),
                  pl.BlockSpec((D_FF, D), lambda b, i: (0, 0), pipeline_mode=once),
                  vec],
        out_specs=row,
        out_shape=jax.ShapeDtypeStruct((B, S, D), F32),
        compiler_params=_cparams(("parallel", "parallel")),
        name="swiglu_ffn",
    )(h, g_pre, w_gu, w_down, g_post)


def _nsa_layer(h, rope, g_pre, g_post, w_in, cmp_pos, wk1, wk2, wv1, wv2, w_o):
    tabs, cos_t, sin_t = rope
    qt, kvc, kvs, kvw, gt = _nsa_proj(h, g_pre, *_nsa_in_weight(w_in))
    kvcmp = _compress(kvc, *_compress_weights(cmp_pos, wk1, wk2, wv1, wv2))
    o = _nsa_attention(qt, cos_t, sin_t, tabs, kvcmp, kvs, kvw, gt)
    return _out_proj(o, w_o.astype(BF16), g_post, h)


def _dil_layer(h, tabs, g_pre, g_post, w_in, w_o):
    D = D_MODEL
    outs, lses = [], []
    for gi, (window, dil) in enumerate(DIL_PATTERNS):
        w = w_in[:, gi * 3 * D:(gi + 1) * 3 * D].astype(BF16)
        q, k, v = _dil_proj(h, tabs, g_pre, w, dil)
        o, lse = _dil_attention(q, k, v, window // dil)
        outs.append(o)
        lses.append(lse)
    return _dil_out(outs, lses, w_o.astype(BF16), g_post, h)


def kernel(x, positions, norm_mix_pre, norm_mix_post, norm_ffn_pre, norm_ffn_post, ffn_w_gu,
           ffn_w_down, nsa_w_in, nsa_cmp_pos, nsa_cmp_wk1, nsa_cmp_wk2, nsa_cmp_wv1,
           nsa_cmp_wv2, nsa_w_o, dil_w_in, dil_w_o):
    depth = norm_mix_pre.shape[0]
    rope = _rope_tables(positions)
    tabs = rope[0]
    vec = lambda g: g.reshape(1, D_MODEL)
    h = x
    for i in range(depth):
        j = i // 2
        if i % 2 == 0:
            h = _nsa_layer(h, rope, vec(norm_mix_pre[i]), vec(norm_mix_post[i]), nsa_w_in[j],
                           nsa_cmp_pos[j], nsa_cmp_wk1[j], nsa_cmp_wk2[j], nsa_cmp_wv1[j],
                           nsa_cmp_wv2[j], nsa_w_o[j])
        else:
            h = _dil_layer(h, tabs, vec(norm_mix_pre[i]), vec(norm_mix_post[i]),
                           dil_w_in[j], dil_w_o[j])
        h = _ffn(h, vec(norm_ffn_pre[i]), ffn_w_gu[i].astype(BF16), ffn_w_down[i].astype(BF16),
                 vec(norm_ffn_post[i]))
    return h
```

```python
import functools

import jax
import jax.numpy as jnp
from jax import lax
from jax.experimental import pallas as pl
from jax.experimental.pallas import tpu as pltpu

F32 = jnp.float32
BF16 = jnp.bfloat16

D_MODEL = 1024
HEAD_DIM = 64
HEAD_SHIFT = 6
ROPE_DIM = HEAD_DIM // 4
ROPE_HALF = ROPE_DIM // 2
ROPE_THETA = 500000.0
NORM_EPS = 1e-6
NEG = -1e30
SCALE = HEAD_DIM ** -0.5
LOG2E = 1.4426950408889634
LN2 = 0.6931471805599453

NSA_HEADS = 16
NSA_KV_HEADS = 4
NSA_GROUP = NSA_HEADS // NSA_KV_HEADS
CMP_BLOCK = 32
CMP_STRIDE = 16
CMP_HIDDEN = 4 * HEAD_DIM
SEL_BLOCK = 64
SEL_SHIFT = 6
SEL_TOPN = 8
FORCE_SCORE = 1e4
WIN = 512
NSA_Q = NSA_HEADS * HEAD_DIM
NSA_KV = NSA_KV_HEADS * 2 * HEAD_DIM
NSA_GATES = 3 * NSA_HEADS
NSA_W_COLS = NSA_Q + 3 * NSA_KV + 128

DIL_PATTERNS = ((128, 1), (512, 4), (2048, 16))
DIL_HEADS = 16
DIL_BAND_BLOCK = 128

D_FF = 2816

LANES = 128
SUBLANES = 8
VMEM_LIMIT = 56 * 1024 * 1024


def _cparams(sem):
    return pltpu.CompilerParams(dimension_semantics=sem, vmem_limit_bytes=VMEM_LIMIT)


def _rms(x, g):
    ms = jnp.mean(x * x, axis=-1, keepdims=True)
    return x * lax.rsqrt(ms + NORM_EPS) * g


def _sigmoid(x):
    return 1.0 / (1.0 + jnp.exp(-x))


def _rope128(x, c, s1, s2):
    return x * c + pltpu.roll(x, ROPE_HALF, 1) * s1 + pltpu.roll(x, LANES - ROPE_HALF, 1) * s2


def _rope_wide(x, c, s1, s2):
    n = x.shape[1] // LANES
    return jnp.concatenate(
        [_rope128(x[:, i * LANES:(i + 1) * LANES], c, s1, s2) for i in range(n)], axis=1)


def _dot(a, b):
    return jnp.dot(a, b, preferred_element_type=F32)


def _dot_t(a, b):
    return lax.dot_general(a, b, (((1,), (1,)), ((), ())), preferred_element_type=F32)


def _rope_tables(positions):
    inv = ROPE_THETA ** (-jnp.arange(0, ROPE_DIM, 2, dtype=F32) / ROPE_DIM)
    ang = positions.astype(F32)[..., None] * inv
    cos, sin = jnp.cos(ang), jnp.sin(ang)
    one = jnp.ones(ang.shape[:-1] + (HEAD_DIM - ROPE_DIM,), F32)
    zero = jnp.zeros_like(one)
    z8 = jnp.zeros_like(sin)
    c64 = jnp.concatenate([cos, cos, one], axis=-1)
    s1 = jnp.concatenate([z8, sin, zero], axis=-1)
    s2 = jnp.concatenate([-sin, z8, zero], axis=-1)
    tile = lambda t: jnp.concatenate([t, t], axis=-1)
    lane_tabs = (tile(c64), tile(s1), tile(s2))
    return lane_tabs, cos.transpose(0, 2, 1), sin.transpose(0, 2, 1)


def _nsa_proj_kernel(h_ref, g_ref, wt_ref, wkv_ref, qt_ref, kvc_ref, kvs_ref, kvw_ref, gt_ref):
    a = _rms(h_ref[0], g_ref[...]).astype(BF16)
    rt = _dot_t(wt_ref[...], a)
    qt_ref[0] = (rt[:NSA_Q] * (SCALE * LOG2E)).astype(BF16)
    gt_ref[0] = _sigmoid(rt[NSA_Q:])
    r = _dot(a, wkv_ref[...])
    for i, ref in enumerate((kvc_ref, kvs_ref, kvw_ref)):
        for hk in range(NSA_KV_HEADS):
            lo = i * NSA_KV + hk * LANES
            ref[0, hk] = r[:, lo:lo + LANES].astype(ref.dtype)


def _nsa_proj(h, g_pre, wt, wkv, tm=512):
    B, S, D = h.shape
    kv_shape = jax.ShapeDtypeStruct((B, NSA_KV_HEADS, S, LANES), BF16)
    kv_spec = pl.BlockSpec((1, NSA_KV_HEADS, tm, LANES), lambda b, i: (b, 0, i, 0))
    return pl.pallas_call(
        _nsa_proj_kernel,
        grid=(B, S // tm),
        in_specs=[
            pl.BlockSpec((1, tm, D), lambda b, i: (b, i, 0)),
            pl.BlockSpec((1, D), lambda b, i: (0, 0)),
            pl.BlockSpec((NSA_Q + LANES, D), lambda b, i: (0, 0)),
            pl.BlockSpec((D, 3 * NSA_KV), lambda b, i: (0, 0)),
        ],
        out_specs=[
            pl.BlockSpec((1, NSA_Q, tm), lambda b, i: (b, 0, i)),
            kv_spec, kv_spec, kv_spec,
            pl.BlockSpec((1, LANES, tm), lambda b, i: (b, 0, i)),
        ],
        out_shape=[
            jax.ShapeDtypeStruct((B, NSA_Q, S), BF16),
            jax.ShapeDtypeStruct(kv_shape.shape, F32),
            kv_shape, kv_shape,
            jax.ShapeDtypeStruct((B, LANES, S), F32),
        ],
        compiler_params=_cparams(("parallel", "parallel")),
        name="nsa_proj",
    )(h, g_pre, wt, wkv)


def _nsa_in_weight(w_in):
    dq, dkv = NSA_Q, NSA_KV_HEADS * HEAD_DIM
    kv = w_in[:, dq:dq + 6 * dkv].reshape(D_MODEL, 3, 2, NSA_KV_HEADS, HEAD_DIM)
    kv = kv.transpose(0, 1, 3, 2, 4).reshape(D_MODEL, 3 * NSA_KV)
    gates = jnp.pad(w_in[:, dq + 6 * dkv:], ((0, 0), (0, LANES - NSA_GATES)))
    wt = jnp.concatenate([w_in[:, :dq], gates], axis=1).T
    return wt.astype(BF16), kv.astype(BF16)


def _compress_kernel(x_ref, pos_ref, wk1_ref, wv1_ref, wk2_ref, wv2_ref, o_ref):
    n = x_ref.shape[2] // CMP_STRIDE
    zrows = jnp.zeros((HEAD_DIM, CMP_HIDDEN), BF16)
    hid = {}
    for r in range(CMP_STRIDE):
        x = x_ref[0, 0, pl.ds(r, n, stride=CMP_STRIDE), :]
        for half in range(2):
            l = half * CMP_STRIDE + r
            xb = (x + pos_ref[l:l + 1, :]).astype(BF16)
            rows = slice(l * HEAD_DIM, (l + 1) * HEAD_DIM)
            for name, w_ref, stack in (("k", wk1_ref, lambda w: [w, zrows]),
                                       ("v", wv1_ref, lambda w: [zrows, w])):
                part = _dot(xb, jnp.concatenate(stack(w_ref[rows, :]), axis=0))
                hid[name, half] = part if r == 0 else hid[name, half] + part
    out = None
    for name, w2_ref in (("k", wk2_ref), ("v", wv2_ref)):
        h = hid[name, 0] + pltpu.roll(hid[name, 1], n - 1, 0)
        part = _dot((h * _sigmoid(h)).astype(BF16), w2_ref[...])
        out = part if out is None else out + part
    o_ref[0, 0] = out


def _compress(kvc, pos2, wk1, wv1, wk2p, wv2p):
    B, HK, S, _ = kvc.shape
    nchunk = S // CMP_STRIDE
    full = lambda x: pl.BlockSpec(x.shape, lambda b, h: (0,) * x.ndim)
    return pl.pallas_call(
        _compress_kernel,
        grid=(B, HK),
        in_specs=[pl.BlockSpec((1, 1, S, LANES), lambda b, h: (b, h, 0, 0)),
                  full(pos2), full(wk1), full(wv1), full(wk2p), full(wv2p)],
        out_specs=pl.BlockSpec((1, 1, nchunk, LANES), lambda b, h: (b, h, 0, 0)),
        out_shape=jax.ShapeDtypeStruct((B, HK, nchunk, LANES), F32),
        compiler_params=_cparams(("parallel", "parallel")),
        name="nsa_compress",
    )(kvc, pos2, wk1, wv1, wk2p, wv2p)


def _compress_weights(cmp_pos, wk1, wk2, wv1, wv2):
    pos2 = jnp.concatenate([cmp_pos, cmp_pos], axis=1)
    wk2p = jnp.pad(wk2, ((0, 0), (0, HEAD_DIM)))
    wv2p = jnp.pad(wv2, ((0, 0), (HEAD_DIM, 0)))
    return pos2, wk1.astype(BF16), wv1.astype(BF16), wk2p.astype(BF16), wv2p.astype(BF16)


class _Flash:
    def __init__(self, qst, k_scr, kt_scr, tq, tkv, s_scr, p_scr, acc_scr):
        self.qst, self.k_scr, self.kt_scr = qst, k_scr, kt_scr
        self.tq, self.tkv = tq, tkv
        self.s_scr, self.p_scr, self.acc_scr = s_scr, p_scr, acc_scr
        self.last = kt_scr.shape[0] - 1
        width = NSA_GROUP * tq
        self.m0 = jnp.full((1, width), NEG, F32)
        self.a0 = jnp.ones((1, width), F32)
        acc_scr[...] = jnp.zeros_like(acc_scr)

    def scores(self, j, slot):
        off = pl.multiple_of(jnp.clip(j, 0, self.last) * self.tkv, self.tkv)
        self.s_scr[slot] = _dot(self.k_scr[pl.ds(off, self.tkv), :], self.qst)

    def values(self, j, slot, a):
        vt = self.kt_scr[jnp.clip(j, 0, self.last), HEAD_DIM - SUBLANES:, :]
        self.acc_scr[...] = a * self.acc_scr[...] + _dot(vt, self.p_scr[slot])

    def softmax(self, slot, m, bias=None):
        tq, ms = self.tq, []
        for g in range(NSA_GROUP):
            sl = slice(g * tq, (g + 1) * tq)
            sg = self.s_scr[slot, :, sl]
            if bias is not None:
                sg = sg + bias
            mg = jnp.maximum(m[:, sl], jnp.max(sg, axis=0, keepdims=True))
            self.p_scr[slot, :, sl] = jnp.exp2(sg - mg).astype(BF16)
            ms.append(mg)
        m_new = jnp.concatenate(ms, axis=1)
        return m_new, jnp.exp2(m - m_new)

    def result(self):
        acc = self.acc_scr[...]
        o = acc[SUBLANES:] / acc[SUBLANES - 1:SUBLANES]
        return [o[:, g * self.tq:(g + 1) * self.tq] for g in range(NSA_GROUP)]


def _nsa_attn_kernel(qt_ref, cos_ref, sin_ref, kc_ref, ks1_ref, ks2_ref, kvc_ref, kvs_ref,
                     kvw_ref, gt_ref, o_ref, ks_scr, kst_scr, kw_scr, kwt_scr, s_scr, p_scr,
                     acc_scr, *, tq, tkv):
    G = NSA_GROUP
    hk = pl.program_id(1)
    qi = pl.program_id(2)
    S = kvs_ref.shape[2]
    n_sel = S // SEL_BLOCK

    @pl.when(qi == 0)
    def _():
        is_key = lax.broadcasted_iota(jnp.int32, (S, LANES), 1) < HEAD_DIM
        c = jnp.where(is_key, kc_ref[0], 1.0)
        s1 = jnp.where(is_key, ks1_ref[0], 0.0)
        s2 = jnp.where(is_key, ks2_ref[0], 0.0)
        lane = lax.broadcasted_iota(jnp.int32, (S, LANES), 1)
        blk = jnp.right_shift(lax.broadcasted_iota(jnp.int32, (S, LANES), 0), SEL_SHIFT)
        onehot = jnp.where(lane - HEAD_DIM == blk, 1.0, 0.0)
        for src, scr, scr_t in ((kvs_ref, ks_scr, kst_scr), (kvw_ref, kw_scr, kwt_scr)):
            kr = _rope128(src[0, 0].astype(F32), c, s1, s2)
            scr[...] = (jnp.where(is_key, kr, onehot) if scr is ks_scr else kr).astype(BF16)
            ones_row = lax.broadcasted_iota(jnp.int32, (LANES, tkv), 0) == HEAD_DIM - 1
            for jj in range(S // tkv):
                krt = kr[jj * tkv:(jj + 1) * tkv].T
                scr_t[jj] = jnp.where(ones_row, 1.0, krt).astype(BF16)

    q = qt_ref[0].astype(F32)
    cos, sin = cos_ref[0], sin_ref[0]
    zpad = jnp.zeros((LANES - HEAD_DIM, tq), F32)
    plain, rot = [], []
    for g in range(G):
        x = q[g * HEAD_DIM:(g + 1) * HEAD_DIM]
        x1, x2 = x[:ROPE_HALF], x[ROPE_HALF:ROPE_DIM]
        xr = jnp.concatenate([x1 * cos - x2 * sin, x2 * cos + x1 * sin, x[ROPE_DIM:]], axis=0)
        plain.append(jnp.concatenate([x, zpad], axis=0))
        rot.append(xr)
    qst_plain = jnp.concatenate(plain, axis=1).astype(BF16)
    qst_win = jnp.concatenate([jnp.concatenate([xr, zpad], axis=0) for xr in rot],
                              axis=1).astype(BF16)

    key_r = lax.broadcasted_iota(jnp.int32, (tkv, tq), 0)
    qry = qi * tq + lax.broadcasted_iota(jnp.int32, (tkv, tq), 1)

    def causal_bias(j):
        return jnp.where(j * tkv + key_r <= qry, 0.0, NEG)

    def band_bias(j):
        d = qry - (j * tkv + key_r)
        inside = jnp.where(j >= 0, 0.0, NEG)
        return jnp.where(d >= 0, jnp.where(d <= WIN - 1, inside, NEG), NEG)

    flash_scr = (tq, tkv, s_scr, p_scr, acc_scr)
    assert tq == tkv

    fl = _Flash(qst_win, kw_scr, kwt_scr, *flash_scr)
    win_chunks = (tq + WIN - 1 + tkv - 1) // tkv
    first = qi - (win_chunks - 1)
    m = fl.m0
    fl.scores(first, 0)
    for c in range(win_chunks):
        if c + 1 < win_chunks:
            fl.scores(first + c + 1, (c + 1) % 2)
        m, a = fl.softmax(c % 2, m, band_bias(first + c))
        fl.values(first + c, c % 2, a)
    o_win = fl.result()

    kvc = kvc_ref[0, 0]
    c_idx = lax.broadcasted_iota(jnp.int32, (LANES, tq), 0)
    t_pos = qi * tq + lax.broadcasted_iota(jnp.int32, (LANES, tq), 1)
    c_valid = (CMP_STRIDE * c_idx + CMP_BLOCK - 1) <= t_pos
    c_valid_f = jnp.where(c_valid, 1.0, 0.0)
    s = _dot(kvc.astype(BF16), qst_plain)
    ps = []
    for g in range(G):
        sg = jnp.where(c_valid, s[:, g * tq:(g + 1) * tq], NEG)
        e = jnp.exp2(sg - jnp.max(sg, axis=0, keepdims=True))
        ps.append(e / jnp.sum(e, axis=0, keepdims=True) * c_valid_f)
    o_cmp = _dot(kvc.T.astype(BF16), jnp.concatenate(ps, axis=1).astype(BF16))

    p_sum = ps[0]
    for g in range(1, G):
        p_sum = p_sum + ps[g]
    jb0 = lax.broadcasted_iota(jnp.int32, (LANES, LANES), 0) * SEL_BLOCK
    cb = lax.broadcasted_iota(jnp.int32, (LANES, LANES), 1) * CMP_STRIDE
    overlap = jnp.where((cb < jb0 + SEL_BLOCK) & (cb + CMP_BLOCK > jb0), 1.0, 0.0).astype(BF16)
    p_hi = p_sum.astype(BF16)
    p_lo = (p_sum - p_hi.astype(F32)).astype(BF16)
    imp = (_dot(overlap, p_hi) + _dot(overlap, p_lo))[:n_sel]
    jrow = lax.broadcasted_iota(jnp.int32, (n_sel, tq), 0)
    tb = jnp.right_shift(qi * tq + lax.broadcasted_iota(jnp.int32, (n_sel, tq), 1), SEL_SHIFT)
    forced = (jrow == 0) | (jrow == tb) | (jrow == tb - 1)
    score = jnp.where(forced, FORCE_SCORE, jnp.where(jrow <= tb, imp, -1.0))
    rank = jnp.zeros((n_sel, tq), F32)
    for j2 in range(n_sel):
        other = score[j2:j2 + 1, :]
        beats = (other > score) | ((other == score) & (jrow > j2))
        rank = rank + jnp.where(beats, 1.0, 0.0)
    sel_bias = jnp.where(rank < min(SEL_TOPN, n_sel), 0.0, NEG)
    zrest = jnp.zeros((LANES - HEAD_DIM - n_sel, tq), F32)
    qst_sel = jnp.concatenate([jnp.concatenate([xr, sel_bias, zrest], axis=0) for xr in rot],
                              axis=1).astype(BF16)

    fl = _Flash(qst_sel, ks_scr, kst_scr, *flash_scr)
    p_scr[1] = jnp.zeros((tkv, G * tq), BF16)
    fl.scores(0, 0)

    def pair(t, carry):
        m, a_prev = carry
        j0 = 2 * t
        fl.scores(j0 + 1, 1)
        fl.values(j0 - 1, 1, a_prev)
        m, a0 = fl.softmax(0, m)
        fl.scores(j0 + 2, 0)
        fl.values(j0, 0, a0)
        return fl.softmax(1, m)

    m, a_prev = lax.fori_loop(0, qi // 2, pair, (fl.m0, fl.a0))
    j0 = 2 * (qi // 2)
    fl.scores(j0 + 1, 1)
    fl.values(j0 - 1, 1, a_prev)
    m, a0 = fl.softmax(0, m, causal_bias(j0))
    fl.values(j0, 0, a0)
    m, a1 = fl.softmax(1, m, causal_bias(j0 + 1))
    fl.values(j0 + 1, 1, a1)
    o_sel = fl.result()

    heads = []
    for g in range(G):
        sl = slice(g * tq, (g + 1) * tq)
        gate = lambda i: gt_ref[0, pl.ds(i * NSA_HEADS + hk * G + g, 1), :]
        heads.append(gate(0) * o_cmp[HEAD_DIM:, sl] + gate(1) * o_sel[g] + gate(2) * o_win[g])
    o_ref[0] = jnp.concatenate(heads, axis=0).T.astype(o_ref.dtype)


def _nsa_attention(qt, cos_t, sin_t, tabs, kvcmp, kvs, kvw, gt, tq=256, tkv=256):
    B, _, S = qt.shape
    HK = NSA_KV_HEADS
    gw = NSA_GROUP * HEAD_DIM
    tab_q = pl.BlockSpec((1, ROPE_HALF, tq), lambda b, h, i: (b, 0, i))
    tab_k = pl.BlockSpec((1, S, LANES), lambda b, h, i: (b, 0, 0))
    kv_spec = pl.BlockSpec((1, 1, S, LANES), lambda b, h, i: (b, h, 0, 0))
    kv_scr = pltpu.VMEM((S, LANES), BF16)
    kvt_scr = pltpu.VMEM((S // tkv, LANES, tkv), BF16)
    return pl.pallas_call(
        functools.partial(_nsa_attn_kernel, tq=tq, tkv=tkv),
        grid=(B, HK, S // tq),
        in_specs=[
            pl.BlockSpec((1, gw, tq), lambda b, h, i: (b, h, i)),
            tab_q, tab_q, tab_k, tab_k, tab_k,
            pl.BlockSpec((1, 1, kvcmp.shape[2], LANES), lambda b, h, i: (b, h, 0, 0)),
            kv_spec, kv_spec,
            pl.BlockSpec((1, LANES, tq), lambda b, h, i: (b, 0, i)),
        ],
        out_specs=pl.BlockSpec((1, tq, gw), lambda b, h, i: (b, i, h)),
        out_shape=jax.ShapeDtypeStruct((B, S, NSA_Q), BF16),
        scratch_shapes=[kv_scr, kvt_scr, kv_scr, kvt_scr,
                        pltpu.VMEM((2, tkv, NSA_GROUP * tq), F32),
                        pltpu.VMEM((2, tkv, NSA_GROUP * tq), BF16),
                        pltpu.VMEM((SUBLANES + HEAD_DIM, NSA_GROUP * tq), F32)],
        compiler_params=_cparams(("parallel", "parallel", "arbitrary")),
        name="nsa_attention",
    )(qt, cos_t, sin_t, *tabs, kvcmp, kvs, kvw, gt)


def _out_proj_kernel(o_ref, w_ref, g_ref, h_ref, y_ref):
    m = _dot(o_ref[0], w_ref[...])
    y_ref[0] = h_ref[0] + _rms(m, g_ref[...])


def _out_proj(o, w_o, g_post, h, tm=512):
    B, S, D = h.shape
    row = pl.BlockSpec((1, tm, D), lambda b, i: (b, i, 0))
    return pl.pallas_call(
        _out_proj_kernel,
        grid=(B, S // tm),
        in_specs=[row, pl.BlockSpec((D, D), lambda b, i: (0, 0)),
                  pl.BlockSpec((1, D), lambda b, i: (0, 0)), row],
        out_specs=row,
        out_shape=jax.ShapeDtypeStruct((B, S, D), F32),
        compiler_params=_cparams(("parallel", "parallel")),
        name="out_proj",
    )(o, w_o, g_post, h)


def _dil_out_kernel(o0_ref, o1_ref, o2_ref, l0_ref, l1_ref, l2_ref, w_ref, g_ref, h_ref, y_ref,
                    *scratch):
    for ref, scr in zip((o0_ref, o1_ref, o2_ref, l0_ref, l1_ref, l2_ref), scratch):
        dil, rows = ref.shape[1], ref.shape[2]
        for r in range(dil):
            rows_r = slice(None) if dil == 1 else pl.ds(r, rows, stride=dil)
            for ct in range(ref.shape[3] // LANES):
                scr[ct, rows_r, :] = ref[0, r, :, ct * LANES:(ct + 1) * LANES].astype(F32)
    outs, lses = scratch[:3], [s[0] for s in scratch[3:]]
    mx = jnp.maximum(jnp.maximum(lses[0], lses[1]), lses[2])
    ex = [jnp.exp(l - mx) for l in lses]
    den = ex[0] + ex[1] + ex[2]
    alphas = [e / den for e in ex]
    D = h_ref.shape[2]
    col_head = jnp.right_shift(lax.broadcasted_iota(jnp.int32, (LANES, D), 1), HEAD_SHIFT)
    spread = jnp.where(col_head == lax.broadcasted_iota(jnp.int32, (LANES, D), 0),
                       1.0, 0.0).astype(BF16)
    wide = []
    for a in alphas:
        hi = a.astype(BF16)
        wide.append(_dot(hi, spread) + _dot((a - hi.astype(F32)).astype(BF16), spread))
    slabs = []
    for c in range(D // LANES):
        cols = slice(c * LANES, (c + 1) * LANES)
        acc = wide[0][:, cols] * outs[0][c]
        for n in range(1, 3):
            acc = acc + wide[n][:, cols] * outs[n][c]
        slabs.append(acc.astype(BF16))
    o = jnp.concatenate(slabs, axis=1)
    y_ref[0] = h_ref[0] + _rms(_dot(o, w_ref[...]), g_ref[...])


def _dil_out(outs, lses, w_o, g_post, h, tm=512):
    B, S, D = h.shape
    row = pl.BlockSpec((1, tm, D), lambda b, i: (b, i, 0))
    cls = lambda x: pl.BlockSpec((1, x.shape[1], tm // x.shape[1], x.shape[3]),
                                 lambda b, i: (b, 0, i, 0))
    return pl.pallas_call(
        _dil_out_kernel,
        grid=(B, S // tm),
        in_specs=[cls(x) for x in (*outs, *lses)] + [
            pl.BlockSpec((D, D), lambda b, i: (0, 0)),
            pl.BlockSpec((1, D), lambda b, i: (0, 0)), row],
        out_specs=row,
        out_shape=jax.ShapeDtypeStruct((B, S, D), F32),
        scratch_shapes=([pltpu.VMEM((D // LANES, tm, LANES), F32)] * 3
                        + [pltpu.VMEM((1, tm, LANES), F32)] * 3),
        compiler_params=_cparams(("parallel", "parallel")),
        name="dil_out_proj",
    )(*outs, *lses, w_o, g_post, h)


def _class_major(ref, dil):
    rows = ref.shape[0] // dil
    return jnp.concatenate([ref[pl.ds(r, rows, stride=dil), :] for r in range(dil)], axis=0)


def _dil_proj_kernel(h_ref, c_ref, s1_ref, s2_ref, g_ref, w_ref, q_ref, k_ref, v_ref, x_scr,
                     *, dil):
    D = D_MODEL
    rows = h_ref.shape[1] // dil
    if dil == 1:
        x = h_ref[0]
        c, s1, s2 = c_ref[0], s1_ref[0], s2_ref[0]
    else:
        for ct in range(D // LANES):
            x_scr[ct] = h_ref[0, :, ct * LANES:(ct + 1) * LANES]
        x = jnp.concatenate([_class_major(x_scr.at[ct], dil) for ct in range(D // LANES)],
                            axis=1)
        c, s1, s2 = (_class_major(ref.at[0], dil) for ref in (c_ref, s1_ref, s2_ref))
    a = _rms(x, g_ref[...]).astype(BF16)
    w = w_ref[...]
    q = _rope_wide(_dot(a, w[:, :D]), c, s1, s2) * (SCALE * LOG2E)
    k = _rope_wide(_dot(a, w[:, D:2 * D]), c, s1, s2)
    v = _dot(a, w[:, 2 * D:])
    for r in range(dil):
        sl = slice(r * rows, (r + 1) * rows)
        q_ref[0, r] = q[sl].astype(BF16)
        k_ref[0, r] = k[sl].astype(BF16)
        v_ref[0, r] = v[sl].astype(BF16)


def _dil_proj(h, tabs, g_pre, w, dil, tm=512):
    B, S, D = h.shape
    out_shape = jax.ShapeDtypeStruct((B, dil, S // dil, D), BF16)
    out_spec = pl.BlockSpec((1, dil, tm // dil, D), lambda b, i: (b, 0, i, 0))
    tab_spec = pl.BlockSpec((1, tm, LANES), lambda b, i: (b, i, 0))
    return pl.pallas_call(
        functools.partial(_dil_proj_kernel, dil=dil),
        grid=(B, S // tm),
        in_specs=[
            pl.BlockSpec((1, tm, D), lambda b, i: (b, i, 0)),
            tab_spec, tab_spec, tab_spec,
            pl.BlockSpec((1, D), lambda b, i: (0, 0)),
            pl.BlockSpec((D, 3 * D), lambda b, i: (0, 0)),
        ],
        out_specs=[out_spec, out_spec, out_spec],
        out_shape=[out_shape, out_shape, out_shape],
        scratch_shapes=[pltpu.VMEM((D // LANES, tm, LANES), F32)],
        compiler_params=_cparams(("parallel", "parallel")),
        name=f"dil_proj_d{dil}",
    )(h, *tabs, g_pre, w)


DIL_SUB = 2


def _dil_attn_kernel(q_ref, kp_ref, kc_ref, vp_ref, vc_ref, o_ref, lse_ref, *, band, has_prev):
    tq = DIL_BAND_BLOCK
    i = pl.program_id(2)
    nk = 2 * tq if has_prev else tq
    key = lax.broadcasted_iota(jnp.int32, (nk, tq), 0)
    rel = lax.broadcasted_iota(jnp.int32, (nk, tq), 1) + (nk - tq) - key
    mask = (rel >= 0) & (rel <= band)
    tr = lambda x: x.astype(F32).T.astype(BF16)
    zpad = jnp.zeros((HEAD_DIM, tq), BF16)

    subs = []
    if has_prev:
        qt = tr(q_ref[0, 0])
        k_all = jnp.concatenate([kp_ref[0, 0], kc_ref[0, 0]], axis=0)
        vt_all = jnp.concatenate([tr(vp_ref[0, 0]), tr(vc_ref[0, 0])], axis=1)
        first = mask & ((key >= tq) | (i > 0))
        for u in range(DIL_SUB):
            rows = slice(u * tq, (u + 1) * tq)
            subs.append((qt[:, rows], k_all[u * tq:u * tq + nk], vt_all[:, u * tq:u * tq + nk],
                         first if u == 0 else mask, (0, rows)))
    else:
        for u in range(DIL_SUB):
            subs.append((tr(q_ref[0, u]), kc_ref[0, u], tr(vc_ref[0, u]), mask,
                         (u, slice(None))))

    scores = []
    for qt_u, k_u, _, _, _ in subs:
        for h in range(DIL_HEADS):
            qh = qt_u[h * HEAD_DIM:(h + 1) * HEAD_DIM]
            pair = jnp.concatenate([qh, zpad] if h % 2 == 0 else [zpad, qh], axis=0)
            scores.append(_dot(k_u[:, (h // 2) * LANES:(h // 2 + 1) * LANES], pair))
    probs, stats = [], []
    for n, s in enumerate(scores):
        s = jnp.where(subs[n // DIL_HEADS][3], s, NEG)
        m = jnp.max(s, axis=0, keepdims=True)
        p = jnp.exp2(s - m)
        probs.append(p.astype(BF16))
        stats.append((m, jnp.sum(p, axis=0, keepdims=True)))
    for u, (_, _, vt_u, _, (lead, rows)) in enumerate(subs):
        base = u * DIL_HEADS
        outs = [_dot(vt_u[h * HEAD_DIM:(h + 1) * HEAD_DIM], probs[base + h]) / stats[base + h][1]
                for h in range(DIL_HEADS)]
        o_ref[0, lead, rows, :] = jnp.concatenate(outs, axis=0).T.astype(o_ref.dtype)
        lse = [m * LN2 + jnp.log(l) for m, l in stats[base:base + DIL_HEADS]]
        lse.append(jnp.zeros((LANES - DIL_HEADS, tq), F32))
        lse_ref[0, lead, rows, :] = jnp.concatenate(lse, axis=0).T


def _dil_attention(q, k, v, band):
    B, dil, L, D = q.shape
    tq = DIL_BAND_BLOCK
    has_prev = L > tq
    if has_prev:
        grid = (B, dil, L // (DIL_SUB * tq))
        shape = lambda w: (1, 1, DIL_SUB * tq, w)
        cur = lambda w: pl.BlockSpec(shape(w), lambda b, r, i: (b, r, i, 0))
    else:
        grid = (B, dil // DIL_SUB, 1)
        shape = lambda w: (1, DIL_SUB, tq, w)
        cur = lambda w: pl.BlockSpec(shape(w), lambda b, r, i: (b, r, 0, 0))
    prev = pl.BlockSpec((1, 1, tq, D), lambda b, r, i: (b, r, jnp.maximum(DIL_SUB * i - 1, 0), 0))
    return pl.pallas_call(
        functools.partial(_dil_attn_kernel, band=band, has_prev=has_prev),
        grid=grid,
        in_specs=[cur(D), prev, cur(D), prev, cur(D)],
        out_specs=[cur(D), cur(LANES)],
        out_shape=[jax.ShapeDtypeStruct((B, dil, L, D), BF16),
                   jax.ShapeDtypeStruct((B, dil, L, LANES), F32)],
        compiler_params=_cparams(("parallel", "parallel", "arbitrary")),
        name=f"dil_attn_d{dil}",
    )(q, k, k, v, v)


FFN_CHUNK = 256


def _ffn_kernel(h_ref, gpre_ref, wgu_ref, wd_ref, gpost_ref, y_ref):
    a = _rms(h_ref[0], gpre_ref[...]).astype(BF16)
    acc = None
    for j in range(D_FF // FFN_CHUNK):
        cols = slice(j * FFN_CHUNK, (j + 1) * FFN_CHUNK)
        ucols = slice(D_FF + j * FFN_CHUNK, D_FF + (j + 1) * FFN_CHUNK)
        g = _dot(a, wgu_ref[:, cols])
        u = _dot(a, wgu_ref[:, ucols])
        mid = (g * _sigmoid(g) * u).astype(BF16)
        part = _dot(mid, wd_ref[cols, :])
        acc = part if acc is None else acc + part
    y_ref[0] = h_ref[0] + _rms(acc, gpost_ref[...])


def _ffn(h, g_pre, w_gu, w_down, g_post, tm=512):
    B, S, D = h.shape
    row = pl.BlockSpec((1, tm, D), lambda b, i: (b, i, 0))
    vec = pl.BlockSpec((1, D), lambda b, i: (0, 0))
    once = pl.Buffered(1)
    return pl.pallas_call(
        _ffn_kernel,
        grid=(B, S // tm),
        in_specs=[row, vec,
                  pl.BlockSpec((D, 2 * D_FF), lambda b, i: (0, 0), pipeline_mode=once),
                  pl.BlockSpec((D_FF, D), lambda b, i: (0, 0), pipeline_mode=once),
                  vec],
        out_specs=row,
        out_shape=jax.ShapeDtypeStruct((B, S, D), F32),
        compiler_params=_cparams(("parallel", "parallel")),
        name="swiglu_ffn",
    )(h, g_pre, w_gu, w_down, g_post)


def _nsa_layer(h, rope, g_pre, g_post, w_in, cmp_pos, wk1, wk2, wv1, wv2, w_o):
    tabs, cos_t, sin_t = rope
    qt, kvc, kvs, kvw, gt = _nsa_proj(h, g_pre, *_nsa_in_weight(w_in))
    kvcmp = _compress(kvc, *_compress_weights(cmp_pos, wk1, wk2, wv1, wv2))
    o = _nsa_attention(qt, cos_t, sin_t, tabs, kvcmp, kvs, kvw, gt)
    return _out_proj(o, w_o.astype(BF16), g_post, h)


def _dil_layer(h, tabs, g_pre, g_post, w_in, w_o):
    D = D_MODEL
    outs, lses = [], []
    for gi, (window, dil) in enumerate(DIL_PATTERNS):
        w = w_in[:, gi * 3 * D:(gi + 1) * 3 * D].astype(BF16)
        q, k, v = _dil_proj(h, tabs, g_pre, w, dil)
        o, lse = _dil_attention(q, k, v, window // dil)
        outs.append(o)
        lses.append(lse)
    return _dil_out(outs, lses, w_o.astype(BF16), g_post, h)


def kernel(x, positions, norm_mix_pre, norm_mix_post, norm_ffn_pre, norm_ffn_post, ffn_w_gu,
           ffn_w_down, nsa_w_in, nsa_cmp_pos, nsa_cmp_wk1, nsa_cmp_wk2, nsa_cmp_wv1,
           nsa_cmp_wv2, nsa_w_o, dil_w_in, dil_w_o):
    depth = norm_mix_pre.shape[0]
    rope = _rope_tables(positions)
    tabs = rope[0]
    vec = lambda g: g.reshape(1, D_MODEL)
    h = x
    for i in range(depth):
        j = i // 2
        if i % 2 == 0:
            h = _nsa_layer(h, rope, vec(norm_mix_pre[i]), vec(norm_mix_post[i]), nsa_w_in[j],
                           nsa_cmp_pos[j], nsa_cmp_wk1[j], nsa_cmp_wk2[j], nsa_cmp_wv1[j],
                           nsa_cmp_wv2[j], nsa_w_o[j])
        else:
            h = _dil_layer(h, tabs, vec(norm_mix_pre[i]), vec(norm_mix_post[i]),
                           dil_w_in[j], dil_w_o[j])
        h = _ffn(h, vec(norm_ffn_pre[i]), ffn_w_gu[i].astype(BF16), ffn_w_down[i].astype(BF16),
                 vec(norm_ffn_post[i]))
    return h
```

```python
import functools

import jax
import jax.numpy as jnp
from jax import lax
from jax.experimental import pallas as pl
from jax.experimental.pallas import tpu as pltpu

F32 = jnp.float32
BF16 = jnp.bfloat16

D_MODEL = 1024
HEAD_DIM = 64
HEAD_SHIFT = 6
ROPE_DIM = HEAD_DIM // 4
ROPE_HALF = ROPE_DIM // 2
ROPE_THETA = 500000.0
NORM_EPS = 1e-6
NEG = -1e30
SCALE = HEAD_DIM ** -0.5
LOG2E = 1.4426950408889634
LN2 = 0.6931471805599453

NSA_HEADS = 16
NSA_KV_HEADS = 4
NSA_GROUP = NSA_HEADS // NSA_KV_HEADS
CMP_BLOCK = 32
CMP_STRIDE = 16
CMP_HIDDEN = 4 * HEAD_DIM
SEL_BLOCK = 64
SEL_SHIFT = 6
SEL_TOPN = 8
FORCE_SCORE = 1e4
WIN = 512
NSA_Q = NSA_HEADS * HEAD_DIM
NSA_KV = NSA_KV_HEADS * 2 * HEAD_DIM
NSA_GATES = 3 * NSA_HEADS
NSA_W_COLS = NSA_Q + 3 * NSA_KV + 128

DIL_PATTERNS = ((128, 1), (512, 4), (2048, 16))
DIL_HEADS = 16
DIL_BAND_BLOCK = 128

D_FF = 2816

LANES = 128
SUBLANES = 8
VMEM_LIMIT = 56 * 1024 * 1024


def _cparams(sem):
    return pltpu.CompilerParams(dimension_semantics=sem, vmem_limit_bytes=VMEM_LIMIT)


def _rms(x, g):
    ms = jnp.mean(x * x, axis=-1, keepdims=True)
    return x * lax.rsqrt(ms + NORM_EPS) * g


def _sigmoid(x):
    return 1.0 / (1.0 + jnp.exp(-x))


def _rope128(x, c, s1, s2):
    return x * c + pltpu.roll(x, ROPE_HALF, 1) * s1 + pltpu.roll(x, LANES - ROPE_HALF, 1) * s2


def _rope_wide(x, c, s1, s2):
    n = x.shape[1] // LANES
    return jnp.concatenate(
        [_rope128(x[:, i * LANES:(i + 1) * LANES], c, s1, s2) for i in range(n)], axis=1)


def _dot(a, b):
    return jnp.dot(a, b, preferred_element_type=F32)


def _dot_t(a, b):
    return lax.dot_general(a, b, (((1,), (1,)), ((), ())), preferred_element_type=F32)


def _rope_tables(positions):
    inv = ROPE_THETA ** (-jnp.arange(0, ROPE_DIM, 2, dtype=F32) / ROPE_DIM)
    pos = positions.astype(F32)
    d = jnp.arange(LANES) % HEAD_DIM
    ang = pos[..., None] * inv[d % ROPE_HALF]
    cos, sin = jnp.cos(ang), jnp.sin(ang)
    c = jnp.where(d < ROPE_DIM, cos, 1.0)
    s1 = jnp.where((d >= ROPE_HALF) & (d < ROPE_DIM), sin, 0.0)
    s2 = jnp.where(d < ROPE_HALF, -sin, 0.0)
    ang_t = pos[:, None, :] * inv[None, :, None]
    return (c, s1, s2), jnp.cos(ang_t), jnp.sin(ang_t)


def _nsa_proj_kernel(h_ref, g_ref, wt_ref, wkv_ref, qt_ref, kvc_ref, kvs_ref, kvw_ref, gt_ref):
    a = _rms(h_ref[0], g_ref[...]).astype(BF16)
    rt = _dot_t(wt_ref[...], a)
    qt_ref[0] = (rt[:NSA_Q] * (SCALE * LOG2E)).astype(BF16)
    gt_ref[0] = _sigmoid(rt[NSA_Q:])
    r = _dot(a, wkv_ref[...])
    for i, ref in enumerate((kvc_ref, kvs_ref, kvw_ref)):
        for hk in range(NSA_KV_HEADS):
            lo = i * NSA_KV + hk * LANES
            ref[0, hk] = r[:, lo:lo + LANES].astype(ref.dtype)


def _nsa_proj(h, g_pre, wt, wkv, tm=512):
    B, S, D = h.shape
    kv_shape = jax.ShapeDtypeStruct((B, NSA_KV_HEADS, S, LANES), BF16)
    kv_spec = pl.BlockSpec((1, NSA_KV_HEADS, tm, LANES), lambda b, i: (b, 0, i, 0))
    return pl.pallas_call(
        _nsa_proj_kernel,
        grid=(B, S // tm),
        in_specs=[
            pl.BlockSpec((1, tm, D), lambda b, i: (b, i, 0)),
            pl.BlockSpec((1, D), lambda b, i: (0, 0)),
            pl.BlockSpec((NSA_Q + LANES, D), lambda b, i: (0, 0)),
            pl.BlockSpec((D, 3 * NSA_KV), lambda b, i: (0, 0)),
        ],
        out_specs=[
            pl.BlockSpec((1, NSA_Q, tm), lambda b, i: (b, 0, i)),
            kv_spec, kv_spec, kv_spec,
            pl.BlockSpec((1, LANES, tm), lambda b, i: (b, 0, i)),
        ],
        out_shape=[
            jax.ShapeDtypeStruct((B, NSA_Q, S), BF16),
            jax.ShapeDtypeStruct(kv_shape.shape, F32),
            kv_shape, kv_shape,
            jax.ShapeDtypeStruct((B, LANES, S), F32),
        ],
        compiler_params=_cparams(("parallel", "parallel")),
        name="nsa_proj",
    )(h, g_pre, wt, wkv)


def _nsa_in_weight(w_in):
    dq, dkv = NSA_Q, NSA_KV_HEADS * HEAD_DIM
    kv = w_in[:, dq:dq + 6 * dkv].reshape(D_MODEL, 3, 2, NSA_KV_HEADS, HEAD_DIM)
    kv = kv.transpose(0, 1, 3, 2, 4).reshape(D_MODEL, 3 * NSA_KV)
    gates = jnp.pad(w_in[:, dq + 6 * dkv:], ((0, 0), (0, LANES - NSA_GATES)))
    wt = jnp.concatenate([w_in[:, :dq], gates], axis=1).T
    return wt.astype(BF16), kv.astype(BF16)


def _compress_kernel(x_ref, pos_ref, wk1_ref, wv1_ref, wk2_ref, wv2_ref, o_ref):
    n = x_ref.shape[2] // CMP_STRIDE
    zrows = jnp.zeros((HEAD_DIM, CMP_HIDDEN), BF16)
    hid = {}
    for r in range(CMP_STRIDE):
        x = x_ref[0, 0, pl.ds(r, n, stride=CMP_STRIDE), :]
        for half in range(2):
            l = half * CMP_STRIDE + r
            xb = (x + pos_ref[l:l + 1, :]).astype(BF16)
            rows = slice(l * HEAD_DIM, (l + 1) * HEAD_DIM)
            for name, w_ref, stack in (("k", wk1_ref, lambda w: [w, zrows]),
                                       ("v", wv1_ref, lambda w: [zrows, w])):
                part = _dot(xb, jnp.concatenate(stack(w_ref[rows, :]), axis=0))
                hid[name, half] = part if r == 0 else hid[name, half] + part
    out = None
    for name, w2_ref in (("k", wk2_ref), ("v", wv2_ref)):
        h = hid[name, 0] + pltpu.roll(hid[name, 1], n - 1, 0)
        part = _dot((h * _sigmoid(h)).astype(BF16), w2_ref[...])
        out = part if out is None else out + part
    o_ref[0, 0] = out


def _compress(kvc, pos2, wk1, wv1, wk2p, wv2p):
    B, HK, S, _ = kvc.shape
    nchunk = S // CMP_STRIDE
    full = lambda x: pl.BlockSpec(x.shape, lambda b, h: (0,) * x.ndim)
    return pl.pallas_call(
        _compress_kernel,
        grid=(B, HK),
        in_specs=[pl.BlockSpec((1, 1, S, LANES), lambda b, h: (b, h, 0, 0)),
                  full(pos2), full(wk1), full(wv1), full(wk2p), full(wv2p)],
        out_specs=pl.BlockSpec((1, 1, nchunk, LANES), lambda b, h: (b, h, 0, 0)),
        out_shape=jax.ShapeDtypeStruct((B, HK, nchunk, LANES), F32),
        compiler_params=_cparams(("parallel", "parallel")),
        name="nsa_compress",
    )(kvc, pos2, wk1, wv1, wk2p, wv2p)


def _compress_weights(cmp_pos, wk1, wk2, wv1, wv2):
    pos2 = jnp.concatenate([cmp_pos, cmp_pos], axis=1)
    wk2p = jnp.pad(wk2, ((0, 0), (0, HEAD_DIM)))
    wv2p = jnp.pad(wv2, ((0, 0), (HEAD_DIM, 0)))
    return pos2, wk1.astype(BF16), wv1.astype(BF16), wk2p.astype(BF16), wv2p.astype(BF16)


class _Flash:
    def __init__(self, qst, k_scr, kt_scr, tq, tkv, s_scr, p_scr, acc_scr):
        self.qst, self.k_scr, self.kt_scr = qst, k_scr, kt_scr
        self.tq, self.tkv = tq, tkv
        self.s_scr, self.p_scr, self.acc_scr = s_scr, p_scr, acc_scr
        self.last = kt_scr.shape[0] - 1
        width = NSA_GROUP * tq
        self.m0 = jnp.full((1, width), NEG, F32)
        self.a0 = jnp.ones((1, width), F32)
        acc_scr[...] = jnp.zeros_like(acc_scr)

    def scores(self, j, slot):
        off = pl.multiple_of(jnp.clip(j, 0, self.last) * self.tkv, self.tkv)
        self.s_scr[slot] = _dot(self.k_scr[pl.ds(off, self.tkv), :], self.qst)

    def values(self, j, slot, a):
        vt = self.kt_scr[jnp.clip(j, 0, self.last), HEAD_DIM - SUBLANES:, :]
        self.acc_scr[...] = a * self.acc_scr[...] + _dot(vt, self.p_scr[slot])

    def softmax(self, slot, m, bias=None):
        tq, ms = self.tq, []
        for g in range(NSA_GROUP):
            sl = slice(g * tq, (g + 1) * tq)
            sg = self.s_scr[slot, :, sl]
            if bias is not None:
                sg = sg + bias
            mg = jnp.maximum(m[:, sl], jnp.max(sg, axis=0, keepdims=True))
            self.p_scr[slot, :, sl] = jnp.exp2(sg - mg).astype(BF16)
            ms.append(mg)
        m_new = jnp.concatenate(ms, axis=1)
        return m_new, jnp.exp2(m - m_new)

    def result(self):
        acc = self.acc_scr[...]
        o = acc[SUBLANES:] / acc[SUBLANES - 1:SUBLANES]
        return [o[:, g * self.tq:(g + 1) * self.tq] for g in range(NSA_GROUP)]


def _nsa_attn_kernel(qt_ref, cos_ref, sin_ref, kc_ref, ks1_ref, ks2_ref, kvc_ref, kvs_ref,
                     kvw_ref, gt_ref, o_ref, ks_scr, kst_scr, kw_scr, kwt_scr, s_scr, p_scr,
                     acc_scr, *, tq, tkv):
    G = NSA_GROUP
    hk = pl.program_id(1)
    qi = pl.program_id(2)
    S = kvs_ref.shape[2]
    n_sel = S // SEL_BLOCK

    @pl.when(qi == 0)
    def _():
        is_key = lax.broadcasted_iota(jnp.int32, (S, LANES), 1) < HEAD_DIM
        c = jnp.where(is_key, kc_ref[0], 1.0)
        s1 = jnp.where(is_key, ks1_ref[0], 0.0)
        s2 = jnp.where(is_key, ks2_ref[0], 0.0)
        lane = lax.broadcasted_iota(jnp.int32, (S, LANES), 1)
        blk = jnp.right_shift(lax.broadcasted_iota(jnp.int32, (S, LANES), 0), SEL_SHIFT)
        onehot = jnp.where(lane - HEAD_DIM == blk, 1.0, 0.0)
        for src, scr, scr_t in ((kvs_ref, ks_scr, kst_scr), (kvw_ref, kw_scr, kwt_scr)):
            kr = _rope128(src[0, 0].astype(F32), c, s1, s2)
            scr[...] = (jnp.where(is_key, kr, onehot) if scr is ks_scr else kr).astype(BF16)
            ones_row = lax.broadcasted_iota(jnp.int32, (LANES, tkv), 0) == HEAD_DIM - 1
            for jj in range(S // tkv):
                krt = kr[jj * tkv:(jj + 1) * tkv].T
                scr_t[jj] = jnp.where(ones_row, 1.0, krt).astype(BF16)

    q = qt_ref[0].astype(F32)
    cos, sin = cos_ref[0], sin_ref[0]
    zpad = jnp.zeros((LANES - HEAD_DIM, tq), F32)
    plain, rot = [], []
    for g in range(G):
        x = q[g * HEAD_DIM:(g + 1) * HEAD_DIM]
        x1, x2 = x[:ROPE_HALF], x[ROPE_HALF:ROPE_DIM]
        xr = jnp.concatenate([x1 * cos - x2 * sin, x2 * cos + x1 * sin, x[ROPE_DIM:]], axis=0)
        plain.append(jnp.concatenate([x, zpad], axis=0))
        rot.append(xr)
    qst_plain = jnp.concatenate(plain, axis=1).astype(BF16)
    qst_win = jnp.concatenate([jnp.concatenate([xr, zpad], axis=0) for xr in rot],
                              axis=1).astype(BF16)

    key_r = lax.broadcasted_iota(jnp.int32, (tkv, tq), 0)
    qry = qi * tq + lax.broadcasted_iota(jnp.int32, (tkv, tq), 1)

    def causal_bias(j):
        return jnp.where(j * tkv + key_r <= qry, 0.0, NEG)

    def band_bias(j):
        d = qry - (j * tkv + key_r)
        inside = jnp.where(j >= 0, 0.0, NEG)
        return jnp.where(d >= 0, jnp.where(d <= WIN - 1, inside, NEG), NEG)

    flash_scr = (tq, tkv, s_scr, p_scr, acc_scr)
    n_diag = tq // tkv
    assert n_diag * tkv == tq and n_diag in (1, 2)
    diag0 = qi * n_diag

    res = {}

    def cmp_scores():
        res["kvc"] = kvc_ref[0, 0]
        res["s"] = _dot(res["kvc"].astype(BF16), qst_plain)

    def cmp_softmax():
        c_idx = lax.broadcasted_iota(jnp.int32, (LANES, tq), 0)
        t_pos = qi * tq + lax.broadcasted_iota(jnp.int32, (LANES, tq), 1)
        c_valid = (CMP_STRIDE * c_idx + CMP_BLOCK - 1) <= t_pos
        c_valid_f = jnp.where(c_valid, 1.0, 0.0)
        ps = []
        for g in range(G):
            sg = jnp.where(c_valid, res["s"][:, g * tq:(g + 1) * tq], NEG)
            e = jnp.exp2(sg - jnp.max(sg, axis=0, keepdims=True))
            ps.append(e / jnp.sum(e, axis=0, keepdims=True) * c_valid_f)
        res["o_cmp"] = _dot(res["kvc"].T.astype(BF16), jnp.concatenate(ps, axis=1).astype(BF16))
        p_sum = ps[0]
        for g in range(1, G):
            p_sum = p_sum + ps[g]
        jb0 = lax.broadcasted_iota(jnp.int32, (LANES, LANES), 0) * SEL_BLOCK
        cb = lax.broadcasted_iota(jnp.int32, (LANES, LANES), 1) * CMP_STRIDE
        overlap = jnp.where((cb < jb0 + SEL_BLOCK) & (cb + CMP_BLOCK > jb0), 1.0, 0.0)
        overlap = overlap.astype(BF16)
        p_hi = p_sum.astype(BF16)
        p_lo = (p_sum - p_hi.astype(F32)).astype(BF16)
        res["imp"] = (_dot(overlap, p_hi) + _dot(overlap, p_lo))[:n_sel]

    def select():
        jrow = lax.broadcasted_iota(jnp.int32, (n_sel, tq), 0)
        tb = jnp.right_shift(qi * tq + lax.broadcasted_iota(jnp.int32, (n_sel, tq), 1),
                             SEL_SHIFT)
        forced = (jrow == 0) | (jrow == tb) | (jrow == tb - 1)
        score = jnp.where(forced, FORCE_SCORE, jnp.where(jrow <= tb, res["imp"], -1.0))
        rank = jnp.zeros((n_sel, tq), F32)
        for j2 in range(n_sel):
            other = score[j2:j2 + 1, :]
            beats = (other > score) | ((other == score) & (jrow > j2))
            rank = rank + jnp.where(beats, 1.0, 0.0)
        sel_bias = jnp.where(rank < min(SEL_TOPN, n_sel), 0.0, NEG)
        zrest = jnp.zeros((LANES - HEAD_DIM - n_sel, tq), F32)
        res["qst_sel"] = jnp.concatenate(
            [jnp.concatenate([xr, sel_bias, zrest], axis=0) for xr in rot], axis=1).astype(BF16)

    chain = [cmp_softmax, select]

    cmp_scores()
    fl = _Flash(qst_win, kw_scr, kwt_scr, *flash_scr)
    win_chunks = (tq + WIN - 1 + tkv - 1) // tkv
    first = diag0 + n_diag - win_chunks
    m = fl.m0
    fl.scores(first, 0)
    for c in range(win_chunks):
        if c + 1 < win_chunks:
            fl.scores(first + c + 1, (c + 1) % 2)
        if c < len(chain):
            chain[c]()
        m, a = fl.softmax(c % 2, m, band_bias(first + c))
        fl.values(first + c, c % 2, a)
    for stage in chain[win_chunks:]:
        stage()
    o_win = fl.result()
    o_cmp, qst_sel = res["o_cmp"], res["qst_sel"]

    fl = _Flash(qst_sel, ks_scr, kst_scr, *flash_scr)
    p_scr[1] = jnp.zeros((tkv, G * tq), BF16)
    fl.scores(0, 0)

    def pair(t, carry):
        m, a_prev = carry
        j0 = 2 * t
        fl.scores(j0 + 1, 1)
        fl.values(j0 - 1, 1, a_prev)
        m, a0 = fl.softmax(0, m)
        fl.scores(j0 + 2, 0)
        fl.values(j0, 0, a0)
        return fl.softmax(1, m)

    m, a_prev = lax.fori_loop(0, diag0 // 2, pair, (fl.m0, fl.a0))
    j0 = 2 * (diag0 // 2)
    fl.scores(j0 + 1, 1)
    fl.values(j0 - 1, 1, a_prev)
    m, a0 = fl.softmax(0, m, causal_bias(j0))
    fl.values(j0, 0, a0)
    m, a1 = fl.softmax(1, m, causal_bias(j0 + 1))
    fl.values(j0 + 1, 1, a1)
    o_sel = fl.result()

    heads = []
    for g in range(G):
        sl = slice(g * tq, (g + 1) * tq)
        gate = lambda i: gt_ref[0, pl.ds(i * NSA_HEADS + hk * G + g, 1), :]
        heads.append(gate(0) * o_cmp[HEAD_DIM:, sl] + gate(1) * o_sel[g] + gate(2) * o_win[g])
    o_ref[0] = jnp.concatenate(heads, axis=0).T.astype(o_ref.dtype)


def _nsa_attention(qt, cos_t, sin_t, tabs, kvcmp, kvs, kvw, gt, tq=256, tkv=256):
    B, _, S = qt.shape
    HK = NSA_KV_HEADS
    gw = NSA_GROUP * HEAD_DIM
    tab_q = pl.BlockSpec((1, ROPE_HALF, tq), lambda b, h, i: (b, 0, i))
    tab_k = pl.BlockSpec((1, S, LANES), lambda b, h, i: (b, 0, 0))
    kv_spec = pl.BlockSpec((1, 1, S, LANES), lambda b, h, i: (b, h, 0, 0))
    kv_scr = pltpu.VMEM((S, LANES), BF16)
    kvt_scr = pltpu.VMEM((S // tkv, LANES, tkv), BF16)
    return pl.pallas_call(
        functools.partial(_nsa_attn_kernel, tq=tq, tkv=tkv),
        grid=(B, HK, S // tq),
        in_specs=[
            pl.BlockSpec((1, gw, tq), lambda b, h, i: (b, h, i)),
            tab_q, tab_q, tab_k, tab_k, tab_k,
            pl.BlockSpec((1, 1, kvcmp.shape[2], LANES), lambda b, h, i: (b, h, 0, 0)),
            kv_spec, kv_spec,
            pl.BlockSpec((1, LANES, tq), lambda b, h, i: (b, 0, i)),
        ],
        out_specs=pl.BlockSpec((1, tq, gw), lambda b, h, i: (b, i, h)),
        out_shape=jax.ShapeDtypeStruct((B, S, NSA_Q), BF16),
        scratch_shapes=[kv_scr, kvt_scr, kv_scr, kvt_scr,
                        pltpu.VMEM((2, tkv, NSA_GROUP * tq), F32),
                        pltpu.VMEM((2, tkv, NSA_GROUP * tq), BF16),
                        pltpu.VMEM((SUBLANES + HEAD_DIM, NSA_GROUP * tq), F32)],
        compiler_params=_cparams(("parallel", "parallel", "arbitrary")),
        name="nsa_attention",
    )(qt, cos_t, sin_t, *tabs, kvcmp, kvs, kvw, gt)


def _out_proj_kernel(o_ref, w_ref, g_ref, h_ref, y_ref):
    m = _dot(o_ref[0], w_ref[...])
    y_ref[0] = h_ref[0] + _rms(m, g_ref[...])


def _out_proj(o, w_o, g_post, h, tm=512):
    B, S, D = h.shape
    row = pl.BlockSpec((1, tm, D), lambda b, i: (b, i, 0))
    return pl.pallas_call(
        _out_proj_kernel,
        grid=(B, S // tm),
        in_specs=[row, pl.BlockSpec((D, D), lambda b, i: (0, 0)),
                  pl.BlockSpec((1, D), lambda b, i: (0, 0)), row],
        out_specs=row,
        out_shape=jax.ShapeDtypeStruct((B, S, D), F32),
        compiler_params=_cparams(("parallel", "parallel")),
        name="out_proj",
    )(o, w_o, g_post, h)


def _dil_out_kernel(o0_ref, o1_ref, o2_ref, l0_ref, l1_ref, l2_ref, w_ref, g_ref, h_ref, y_ref,
                    *scratch):
    for ref, scr in zip((o0_ref, o1_ref, o2_ref, l0_ref, l1_ref, l2_ref), scratch):
        dil, rows = ref.shape[1], ref.shape[2]
        for r in range(dil):
            rows_r = slice(None) if dil == 1 else pl.ds(r, rows, stride=dil)
            for ct in range(ref.shape[3] // LANES):
                scr[ct, rows_r, :] = ref[0, r, :, ct * LANES:(ct + 1) * LANES].astype(F32)
    outs, lses = scratch[:3], [s[0] for s in scratch[3:]]
    mx = jnp.maximum(jnp.maximum(lses[0], lses[1]), lses[2])
    ex = [jnp.exp(l - mx) for l in lses]
    den = ex[0] + ex[1] + ex[2]
    alphas = [e / den for e in ex]
    D = h_ref.shape[2]
    col_head = jnp.right_shift(lax.broadcasted_iota(jnp.int32, (LANES, D), 1), HEAD_SHIFT)
    spread = jnp.where(col_head == lax.broadcasted_iota(jnp.int32, (LANES, D), 0),
                       1.0, 0.0).astype(BF16)
    wide = []
    for a in alphas:
        hi = a.astype(BF16)
        wide.append(_dot(hi, spread) + _dot((a - hi.astype(F32)).astype(BF16), spread))
    slabs = []
    for c in range(D // LANES):
        cols = slice(c * LANES, (c + 1) * LANES)
        acc = wide[0][:, cols] * outs[0][c]
        for n in range(1, 3):
            acc = acc + wide[n][:, cols] * outs[n][c]
        slabs.append(acc.astype(BF16))
    o = jnp.concatenate(slabs, axis=1)
    y_ref[0] = h_ref[0] + _rms(_dot(o, w_ref[...]), g_ref[...])


def _dil_out(outs, lses, w_o, g_post, h, tm=512):
    B, S, D = h.shape
    row = pl.BlockSpec((1, tm, D), lambda b, i: (b, i, 0))
    cls = lambda x: pl.BlockSpec((1, x.shape[1], tm // x.shape[1], x.shape[3]),
                                 lambda b, i: (b, 0, i, 0))
    return pl.pallas_call(
        _dil_out_kernel,
        grid=(B, S // tm),
        in_specs=[cls(x) for x in (*outs, *lses)] + [
            pl.BlockSpec((D, D), lambda b, i: (0, 0)),
            pl.BlockSpec((1, D), lambda b, i: (0, 0)), row],
        out_specs=row,
        out_shape=jax.ShapeDtypeStruct((B, S, D), F32),
        scratch_shapes=([pltpu.VMEM((D // LANES, tm, LANES), F32)] * 3
                        + [pltpu.VMEM((1, tm, LANES), F32)] * 3),
        compiler_params=_cparams(("parallel", "parallel")),
        name="dil_out_proj",
    )(*outs, *lses, w_o, g_post, h)


def _class_major(ref, dil):
    rows = ref.shape[0] // dil
    return jnp.concatenate([ref[pl.ds(r, rows, stride=dil), :] for r in range(dil)], axis=0)


def _dil_proj_kernel(h_ref, c_ref, s1_ref, s2_ref, g_ref, w_ref, q_ref, k_ref, v_ref, x_scr,
                     *, dil):
    D = D_MODEL
    rows = h_ref.shape[1] // dil
    if dil == 1:
        x = h_ref[0]
        c, s1, s2 = c_ref[0], s1_ref[0], s2_ref[0]
    else:
        for ct in range(D // LANES):
            x_scr[ct] = h_ref[0, :, ct * LANES:(ct + 1) * LANES]
        x = jnp.concatenate([_class_major(x_scr.at[ct], dil) for ct in range(D // LANES)],
                            axis=1)
        c, s1, s2 = (_class_major(ref.at[0], dil) for ref in (c_ref, s1_ref, s2_ref))
    a = _rms(x, g_ref[...]).astype(BF16)
    w = w_ref[...]
    q = _rope_wide(_dot(a, w[:, :D]), c, s1, s2) * (SCALE * LOG2E)
    k = _rope_wide(_dot(a, w[:, D:2 * D]), c, s1, s2)
    v = _dot(a, w[:, 2 * D:])
    for r in range(dil):
        sl = slice(r * rows, (r + 1) * rows)
        q_ref[0, r] = q[sl].astype(BF16)
        k_ref[0, r] = k[sl].astype(BF16)
        v_ref[0, r] = v[sl].astype(BF16)


def _dil_proj(h, tabs, g_pre, w, dil, tm=512):
    B, S, D = h.shape
    out_shape = jax.ShapeDtypeStruct((B, dil, S // dil, D), BF16)
    out_spec = pl.BlockSpec((1, dil, tm // dil, D), lambda b, i: (b, 0, i, 0))
    tab_spec = pl.BlockSpec((1, tm, LANES), lambda b, i: (b, i, 0))
    return pl.pallas_call(
        functools.partial(_dil_proj_kernel, dil=dil),
        grid=(B, S // tm),
        in_specs=[
            pl.BlockSpec((1, tm, D), lambda b, i: (b, i, 0)),
            tab_spec, tab_spec, tab_spec,
            pl.BlockSpec((1, D), lambda b, i: (0, 0)),
            pl.BlockSpec((D, 3 * D), lambda b, i: (0, 0)),
        ],
        out_specs=[out_spec, out_spec, out_spec],
        out_shape=[out_shape, out_shape, out_shape],
        scratch_shapes=[pltpu.VMEM((D // LANES, tm, LANES), F32)],
        compiler_params=_cparams(("parallel", "parallel")),
        name=f"dil_proj_d{dil}",
    )(h, *tabs, g_pre, w)


DIL_SUB = 2


def _dil_attn_kernel(q_ref, kp_ref, kc_ref, vp_ref, vc_ref, o_ref, lse_ref, *, band, has_prev):
    tq = DIL_BAND_BLOCK
    i = pl.program_id(2)
    nk = 2 * tq if has_prev else tq
    key = lax.broadcasted_iota(jnp.int32, (nk, tq), 0)
    rel = lax.broadcasted_iota(jnp.int32, (nk, tq), 1) + (nk - tq) - key
    mask = jnp.where(rel >= 0, jnp.where(rel <= band, 0.0, NEG), NEG)
    tr = lambda x: x.T
    zpad = jnp.zeros((HEAD_DIM, tq), BF16)

    subs = []
    if has_prev:
        qt = tr(q_ref[0, 0])
        k_all = jnp.concatenate([kp_ref[0, 0], kc_ref[0, 0]], axis=0)
        vt_all = jnp.concatenate([tr(vp_ref[0, 0]), tr(vc_ref[0, 0])], axis=1)
        before = jnp.where(i > 0, 0.0, NEG)
        first = jnp.where(key >= tq, mask, mask + before)
        for u in range(DIL_SUB):
            rows = slice(u * tq, (u + 1) * tq)
            subs.append((qt[:, rows], k_all[u * tq:u * tq + nk], vt_all[:, u * tq:u * tq + nk],
                         first if u == 0 else mask, (0, rows)))
    else:
        for u in range(DIL_SUB):
            subs.append((tr(q_ref[0, u]), kc_ref[0, u], tr(vc_ref[0, u]), mask,
                         (u, slice(None))))

    scores = []
    for qt_u, k_u, _, _, _ in subs:
        for h in range(DIL_HEADS):
            qh = qt_u[h * HEAD_DIM:(h + 1) * HEAD_DIM]
            pair = jnp.concatenate([qh, zpad] if h % 2 == 0 else [zpad, qh], axis=0)
            scores.append(_dot(k_u[:, (h // 2) * LANES:(h // 2 + 1) * LANES], pair))
    probs, maxes = [], []
    for n, s in enumerate(scores):
        s = s + subs[n // DIL_HEADS][3]
        m = jnp.max(s, axis=0, keepdims=True)
        probs.append(jnp.exp2(s - m).astype(BF16))
        maxes.append(m)
    ones = jnp.ones((SUBLANES, nk), BF16)
    for u, (_, _, vt_u, _, (lead, rows)) in enumerate(subs):
        outs, lse = [], []
        for h in range(DIL_HEADS):
            vt_h = jnp.concatenate([ones, vt_u[h * HEAD_DIM:(h + 1) * HEAD_DIM]], axis=0)
            r = _dot(vt_h, probs[u * DIL_HEADS + h])
            l = r[:1]
            outs.append(r[SUBLANES:] / l)
            lse.append(maxes[u * DIL_HEADS + h] * LN2 + jnp.log(l))
        o_ref[0, lead, rows, :] = jnp.concatenate(outs, axis=0).astype(o_ref.dtype).T
        lse.append(jnp.zeros((LANES - DIL_HEADS, tq), F32))
        lse_ref[0, lead, rows, :] = jnp.concatenate(lse, axis=0).T


def _dil_attention(q, k, v, band):
    B, dil, L, D = q.shape
    tq = DIL_BAND_BLOCK
    has_prev = L > tq
    if has_prev:
        grid = (B, dil, L // (DIL_SUB * tq))
        shape = lambda w: (1, 1, DIL_SUB * tq, w)
        cur = lambda w: pl.BlockSpec(shape(w), lambda b, r, i: (b, r, i, 0))
    else:
        grid = (B, dil // DIL_SUB, 1)
        shape = lambda w: (1, DIL_SUB, tq, w)
        cur = lambda w: pl.BlockSpec(shape(w), lambda b, r, i: (b, r, 0, 0))
    prev = pl.BlockSpec((1, 1, tq, D), lambda b, r, i: (b, r, jnp.maximum(DIL_SUB * i - 1, 0), 0))
    return pl.pallas_call(
        functools.partial(_dil_attn_kernel, band=band, has_prev=has_prev),
        grid=grid,
        in_specs=[cur(D), prev, cur(D), prev, cur(D)],
        out_specs=[cur(D), cur(LANES)],
        out_shape=[jax.ShapeDtypeStruct((B, dil, L, D), BF16),
                   jax.ShapeDtypeStruct((B, dil, L, LANES), F32)],
        compiler_params=_cparams(("parallel", "parallel", "arbitrary")),
        name=f"dil_attn_d{dil}",
    )(q, k, k, v, v)


FFN_CHUNK = 256


def _ffn_kernel(h_ref, gpre_ref, wgu_ref, wd_ref, gpost_ref, y_ref):
    a = _rms(h_ref[0], gpre_ref[...]).astype(BF16)
    acc = None
    for j in range(D_FF // FFN_CHUNK):
        cols = slice(j * FFN_CHUNK, (j + 1) * FFN_CHUNK)
        ucols = slice(D_FF + j * FFN_CHUNK, D_FF + (j + 1) * FFN_CHUNK)
        g = _dot(a, wgu_ref[:, cols])
        u = _dot(a, wgu_ref[:, ucols])
        mid = (g * _sigmoid(g) * u).astype(BF16)
        part = _dot(mid, wd_ref[cols, :])
        acc = part if acc is None else acc + part
    y_ref[0] = h_ref[0] + _rms(acc, gpost_ref[...])


def _ffn(h, g_pre, w_gu, w_down, g_post, tm=512):
    B, S, D = h.shape
    row = pl.BlockSpec((1, tm, D), lambda b, i: (b, i, 0))
    vec = pl.BlockSpec((1, D), lambda b, i: (0, 0))
    once = pl.Buffered(1)
    return pl.pallas_call(
        _ffn_kernel,
        grid=(B, S // tm),
        in_specs=[row, vec,
                  pl.BlockSpec((D, 2 * D_FF), lambda b, i: (0, 0), pipeline_mode=once),
                  pl.BlockSpec((D_FF, D), lambda b, i: (0, 0), pipeline_mode=once),
                  vec],
        out_specs=row,
        out_shape=jax.ShapeDtypeStruct((B, S, D), F32),
        compiler_params=_cparams(("parallel", "parallel")),
        name="swiglu_ffn",
    )(h, g_pre, w_gu, w_down, g_post)


def _nsa_layer(h, rope, g_pre, g_post, w_in, cmp_pos, wk1, wk2, wv1, wv2, w_o):
    tabs, cos_t, sin_t = rope
    qt, kvc, kvs, kvw, gt = _nsa_proj(h, g_pre, *_nsa_in_weight(w_in))
    kvcmp = _compress(kvc, *_compress_weights(cmp_pos, wk1, wk2, wv1, wv2))
    o = _nsa_attention(qt, cos_t, sin_t, tabs, kvcmp, kvs, kvw, gt)
    return _out_proj(o, w_o.astype(BF16), g_post, h)


def _dil_layer(h, tabs, g_pre, g_post, w_in, w_o):
    D = D_MODEL
    outs, lses = [], []
    for gi, (window, dil) in enumerate(DIL_PATTERNS):
        w = w_in[:, gi * 3 * D:(gi + 1) * 3 * D].astype(BF16)
        q, k, v = _dil_proj(h, tabs, g_pre, w, dil)
        o, lse = _dil_attention(q, k, v, window // dil)
        outs.append(o)
        lses.append(lse)
    return _dil_out(outs, lses, w_o.astype(BF16), g_post, h)


def kernel(x, positions, norm_mix_pre, norm_mix_post, norm_ffn_pre, norm_ffn_post, ffn_w_gu,
           ffn_w_down, nsa_w_in, nsa_cmp_pos, nsa_cmp_wk1, nsa_cmp_wk2, nsa_cmp_wv1,
           nsa_cmp_wv2, nsa_w_o, dil_w_in, dil_w_o):
    depth = norm_mix_pre.shape[0]
    rope = _rope_tables(positions)
    tabs = rope[0]
    vec = lambda g: g.reshape(1, D_MODEL)
    h = x
    for i in range(depth):
        j = i // 2
        if i % 2 == 0:
            h = _nsa_layer(h, rope, vec(norm_mix_pre[i]), vec(norm_mix_post[i]), nsa_w_in[j],
                           nsa_cmp_pos[j], nsa_cmp_wk1[j], nsa_cmp_wk2[j], nsa_cmp_wv1[j],
                           nsa_cmp_wv2[j], nsa_w_o[j])
        else:
            h = _dil_layer(h, tabs, vec(norm_mix_pre[i]), vec(norm_mix_post[i]),
                           dil_w_in[j], dil_w_o[j])
        h = _ffn(h, vec(norm_ffn_pre[i]), ffn_w_gu[i].astype(BF16), ffn_w_down[i].astype(BF16),
                 vec(norm_ffn_post[i]))
    return h
```

```python
import functools

import jax
import jax.numpy as jnp
from jax import lax
from jax.experimental import pallas as pl
from jax.experimental.pallas import tpu as pltpu

F32 = jnp.float32
BF16 = jnp.bfloat16

D_MODEL = 1024
HEAD_DIM = 64
HEAD_SHIFT = 6
ROPE_DIM = HEAD_DIM // 4
ROPE_HALF = ROPE_DIM // 2
ROPE_THETA = 500000.0
NORM_EPS = 1e-6
NEG = -1e30
SCALE = HEAD_DIM ** -0.5
LOG2E = 1.4426950408889634
LN2 = 0.6931471805599453

NSA_HEADS = 16
NSA_KV_HEADS = 4
NSA_GROUP = NSA_HEADS // NSA_KV_HEADS
CMP_BLOCK = 32
CMP_STRIDE = 16
CMP_HIDDEN = 4 * HEAD_DIM
SEL_BLOCK = 64
SEL_SHIFT = 6
SEL_TOPN = 8
FORCE_SCORE = 1e4
WIN = 512
NSA_Q = NSA_HEADS * HEAD_DIM
NSA_KV = NSA_KV_HEADS * 2 * HEAD_DIM
NSA_GATES = 3 * NSA_HEADS
NSA_W_COLS = NSA_Q + 3 * NSA_KV + 128

DIL_PATTERNS = ((128, 1), (512, 4), (2048, 16))
DIL_HEADS = 16
DIL_BAND_BLOCK = 128

D_FF = 2816

LANES = 128
SUBLANES = 8
VMEM_LIMIT = 56 * 1024 * 1024


def _cparams(sem):
    return pltpu.CompilerParams(dimension_semantics=sem, vmem_limit_bytes=VMEM_LIMIT)


def _rms(x, g):
    ms = jnp.mean(x * x, axis=-1, keepdims=True)
    return x * lax.rsqrt(ms + NORM_EPS) * g


def _sigmoid(x):
    return 1.0 / (1.0 + jnp.exp(-x))


def _rope128(x, c, s1, s2):
    return x * c + pltpu.roll(x, ROPE_HALF, 1) * s1 + pltpu.roll(x, LANES - ROPE_HALF, 1) * s2


def _rope_wide(x, c, s1, s2):
    n = x.shape[1] // LANES
    return jnp.concatenate(
        [_rope128(x[:, i * LANES:(i + 1) * LANES], c, s1, s2) for i in range(n)], axis=1)


def _dot(a, b):
    return jnp.dot(a, b, preferred_element_type=F32)


def _dot_t(a, b):
    return lax.dot_general(a, b, (((1,), (1,)), ((), ())), preferred_element_type=F32)


def _rope_tables(positions):
    inv = ROPE_THETA ** (-jnp.arange(0, ROPE_DIM, 2, dtype=F32) / ROPE_DIM)
    pos = positions.astype(F32)
    d = jnp.arange(LANES) % HEAD_DIM
    ang = pos[..., None] * inv[d % ROPE_HALF]
    cos, sin = jnp.cos(ang), jnp.sin(ang)
    c = jnp.where(d < ROPE_DIM, cos, 1.0)
    s1 = jnp.where((d >= ROPE_HALF) & (d < ROPE_DIM), sin, 0.0)
    s2 = jnp.where(d < ROPE_HALF, -sin, 0.0)
    ang_t = pos[:, None, :] * inv[None, :, None]
    return (c, s1, s2), jnp.cos(ang_t), jnp.sin(ang_t)


def _nsa_proj_kernel(h_ref, g_ref, wt_ref, wkv_ref, qt_ref, kvc_ref, kvs_ref, kvw_ref, gt_ref):
    a = _rms(h_ref[0], g_ref[...]).astype(BF16)
    rt = _dot_t(wt_ref[...], a)
    qt_ref[0] = (rt[:NSA_Q] * (SCALE * LOG2E)).astype(BF16)
    gt_ref[0] = _sigmoid(rt[NSA_Q:])
    r = _dot(a, wkv_ref[...])
    for i, ref in enumerate((kvc_ref, kvs_ref, kvw_ref)):
        for hk in range(NSA_KV_HEADS):
            lo = i * NSA_KV + hk * LANES
            ref[0, hk] = r[:, lo:lo + LANES].astype(ref.dtype)


def _nsa_proj(h, g_pre, wt, wkv, tm=512):
    B, S, D = h.shape
    kv_shape = jax.ShapeDtypeStruct((B, NSA_KV_HEADS, S, LANES), BF16)
    kv_spec = pl.BlockSpec((1, NSA_KV_HEADS, tm, LANES), lambda b, i: (b, 0, i, 0))
    return pl.pallas_call(
        _nsa_proj_kernel,
        grid=(B, S // tm),
        in_specs=[
            pl.BlockSpec((1, tm, D), lambda b, i: (b, i, 0)),
            pl.BlockSpec((1, D), lambda b, i: (0, 0)),
            pl.BlockSpec((NSA_Q + LANES, D), lambda b, i: (0, 0)),
            pl.BlockSpec((D, 3 * NSA_KV), lambda b, i: (0, 0)),
        ],
        out_specs=[
            pl.BlockSpec((1, NSA_Q, tm), lambda b, i: (b, 0, i)),
            kv_spec, kv_spec, kv_spec,
            pl.BlockSpec((1, LANES, tm), lambda b, i: (b, 0, i)),
        ],
        out_shape=[
            jax.ShapeDtypeStruct((B, NSA_Q, S), BF16),
            jax.ShapeDtypeStruct(kv_shape.shape, F32),
            kv_shape, kv_shape,
            jax.ShapeDtypeStruct((B, LANES, S), F32),
        ],
        compiler_params=_cparams(("parallel", "parallel")),
        name="nsa_proj",
    )(h, g_pre, wt, wkv)


def _nsa_in_weight(w_in):
    dq, dkv = NSA_Q, NSA_KV_HEADS * HEAD_DIM
    kv = w_in[:, dq:dq + 6 * dkv].reshape(D_MODEL, 3, 2, NSA_KV_HEADS, HEAD_DIM)
    kv = kv.transpose(0, 1, 3, 2, 4).reshape(D_MODEL, 3 * NSA_KV)
    gates = jnp.pad(w_in[:, dq + 6 * dkv:], ((0, 0), (0, LANES - NSA_GATES)))
    wt = jnp.concatenate([w_in[:, :dq], gates], axis=1).T
    return wt.astype(BF16), kv.astype(BF16)


def _compress_kernel(x_ref, pos_ref, wk1_ref, wv1_ref, wk2_ref, wv2_ref, o_ref):
    n = x_ref.shape[2] // CMP_STRIDE
    zrows = jnp.zeros((HEAD_DIM, CMP_HIDDEN), BF16)
    hid = {}
    for r in range(CMP_STRIDE):
        x = x_ref[0, 0, pl.ds(r, n, stride=CMP_STRIDE), :]
        for half in range(2):
            l = half * CMP_STRIDE + r
            xb = (x + pos_ref[l:l + 1, :]).astype(BF16)
            rows = slice(l * HEAD_DIM, (l + 1) * HEAD_DIM)
            for name, w_ref, stack in (("k", wk1_ref, lambda w: [w, zrows]),
                                       ("v", wv1_ref, lambda w: [zrows, w])):
                part = _dot(xb, jnp.concatenate(stack(w_ref[rows, :]), axis=0))
                hid[name, half] = part if r == 0 else hid[name, half] + part
    out = None
    for name, w2_ref in (("k", wk2_ref), ("v", wv2_ref)):
        h = hid[name, 0] + pltpu.roll(hid[name, 1], n - 1, 0)
        part = _dot((h * _sigmoid(h)).astype(BF16), w2_ref[...])
        out = part if out is None else out + part
    o_ref[0, 0] = out


def _compress(kvc, pos2, wk1, wv1, wk2p, wv2p):
    B, HK, S, _ = kvc.shape
    nchunk = S // CMP_STRIDE
    full = lambda x: pl.BlockSpec(x.shape, lambda b, h: (0,) * x.ndim)
    return pl.pallas_call(
        _compress_kernel,
        grid=(B, HK),
        in_specs=[pl.BlockSpec((1, 1, S, LANES), lambda b, h: (b, h, 0, 0)),
                  full(pos2), full(wk1), full(wv1), full(wk2p), full(wv2p)],
        out_specs=pl.BlockSpec((1, 1, nchunk, LANES), lambda b, h: (b, h, 0, 0)),
        out_shape=jax.ShapeDtypeStruct((B, HK, nchunk, LANES), F32),
        compiler_params=_cparams(("parallel", "parallel")),
        name="nsa_compress",
    )(kvc, pos2, wk1, wv1, wk2p, wv2p)


def _compress_weights(cmp_pos, wk1, wk2, wv1, wv2):
    pos2 = jnp.concatenate([cmp_pos, cmp_pos], axis=1)
    wk2p = jnp.pad(wk2, ((0, 0), (0, HEAD_DIM)))
    wv2p = jnp.pad(wv2, ((0, 0), (HEAD_DIM, 0)))
    return pos2, wk1.astype(BF16), wv1.astype(BF16), wk2p.astype(BF16), wv2p.astype(BF16)


class _Flash:
    def __init__(self, qst, k_scr, kt_scr, tq, tkv, s_scr, p_scr, acc_scr):
        self.qst, self.k_scr, self.kt_scr = qst, k_scr, kt_scr
        self.tq, self.tkv = tq, tkv
        self.s_scr, self.p_scr, self.acc_scr = s_scr, p_scr, acc_scr
        self.last = kt_scr.shape[0] - 1
        width = NSA_GROUP * tq
        self.m0 = jnp.full((1, width), NEG, F32)
        self.a0 = jnp.ones((1, width), F32)
        acc_scr[...] = jnp.zeros_like(acc_scr)

    def scores(self, j, slot):
        off = pl.multiple_of(jnp.clip(j, 0, self.last) * self.tkv, self.tkv)
        self.s_scr[slot] = _dot(self.k_scr[pl.ds(off, self.tkv), :], self.qst)

    def values(self, j, slot, a):
        vt = self.kt_scr[jnp.clip(j, 0, self.last), HEAD_DIM - SUBLANES:, :]
        self.acc_scr[...] = a * self.acc_scr[...] + _dot(vt, self.p_scr[slot])

    def softmax(self, slot, m, bias=None):
        tq, ms = self.tq, []
        for g in range(NSA_GROUP):
            sl = slice(g * tq, (g + 1) * tq)
            sg = self.s_scr[slot, :, sl]
            if bias is not None:
                sg = sg + bias
            mg = jnp.maximum(m[:, sl], jnp.max(sg, axis=0, keepdims=True))
            self.p_scr[slot, :, sl] = jnp.exp2(sg - mg).astype(BF16)
            ms.append(mg)
        m_new = jnp.concatenate(ms, axis=1)
        return m_new, jnp.exp2(m - m_new)

    def result(self):
        acc = self.acc_scr[...]
        o = acc[SUBLANES:] / acc[SUBLANES - 1:SUBLANES]
        return [o[:, g * self.tq:(g + 1) * self.tq] for g in range(NSA_GROUP)]


def _nsa_attn_kernel(qt_ref, cos_ref, sin_ref, kc_ref, ks1_ref, ks2_ref, kvc_ref, kvs_ref,
                     kvw_ref, gt_ref, o_ref, ks_scr, kst_scr, kw_scr, kwt_scr, s_scr, p_scr,
                     acc_scr, *, tq, tkv):
    G = NSA_GROUP
    hk = pl.program_id(1)
    qi = pl.program_id(2)
    S = kvs_ref.shape[2]
    n_sel = S // SEL_BLOCK

    @pl.when(qi == 0)
    def _():
        is_key = lax.broadcasted_iota(jnp.int32, (S, LANES), 1) < HEAD_DIM
        c = jnp.where(is_key, kc_ref[0], 1.0)
        s1 = jnp.where(is_key, ks1_ref[0], 0.0)
        s2 = jnp.where(is_key, ks2_ref[0], 0.0)
        lane = lax.broadcasted_iota(jnp.int32, (S, LANES), 1)
        blk = jnp.right_shift(lax.broadcasted_iota(jnp.int32, (S, LANES), 0), SEL_SHIFT)
        onehot = jnp.where(lane - HEAD_DIM == blk, 1.0, 0.0)
        for src, scr, scr_t in ((kvs_ref, ks_scr, kst_scr), (kvw_ref, kw_scr, kwt_scr)):
            kr = _rope128(src[0, 0].astype(F32), c, s1, s2)
            scr[...] = (jnp.where(is_key, kr, onehot) if scr is ks_scr else kr).astype(BF16)
            ones_row = lax.broadcasted_iota(jnp.int32, (LANES, tkv), 0) == HEAD_DIM - 1
            for jj in range(S // tkv):
                krt = kr[jj * tkv:(jj + 1) * tkv].T
                scr_t[jj] = jnp.where(ones_row, 1.0, krt).astype(BF16)

    q = qt_ref[0].astype(F32)
    cos, sin = cos_ref[0], sin_ref[0]
    zpad = jnp.zeros((LANES - HEAD_DIM, tq), F32)
    plain, rot = [], []
    for g in range(G):
        x = q[g * HEAD_DIM:(g + 1) * HEAD_DIM]
        x1, x2 = x[:ROPE_HALF], x[ROPE_HALF:ROPE_DIM]
        xr = jnp.concatenate([x1 * cos - x2 * sin, x2 * cos + x1 * sin, x[ROPE_DIM:]], axis=0)
        plain.append(jnp.concatenate([x, zpad], axis=0))
        rot.append(xr)
    qst_plain = jnp.concatenate(plain, axis=1).astype(BF16)
    qst_win = jnp.concatenate([jnp.concatenate([xr, zpad], axis=0) for xr in rot],
                              axis=1).astype(BF16)

    key_r = lax.broadcasted_iota(jnp.int32, (tkv, tq), 0)
    qry = qi * tq + lax.broadcasted_iota(jnp.int32, (tkv, tq), 1)

    def causal_bias(j):
        return jnp.where(j * tkv + key_r <= qry, 0.0, NEG)

    def band_bias(j):
        d = qry - (j * tkv + key_r)
        inside = jnp.where(j >= 0, 0.0, NEG)
        return jnp.where(d >= 0, jnp.where(d <= WIN - 1, inside, NEG), NEG)

    flash_scr = (tq, tkv, s_scr, p_scr, acc_scr)
    n_diag = tq // tkv
    assert n_diag * tkv == tq and n_diag in (1, 2)
    diag0 = qi * n_diag

    res = {}

    def cmp_scores():
        res["kvc"] = kvc_ref[0, 0]
        res["s"] = _dot(res["kvc"].astype(BF16), qst_plain)

    def cmp_softmax():
        c_idx = lax.broadcasted_iota(jnp.int32, (LANES, tq), 0)
        t_pos = qi * tq + lax.broadcasted_iota(jnp.int32, (LANES, tq), 1)
        c_valid = (CMP_STRIDE * c_idx + CMP_BLOCK - 1) <= t_pos
        c_valid_f = jnp.where(c_valid, 1.0, 0.0)
        ps = []
        for g in range(G):
            sg = jnp.where(c_valid, res["s"][:, g * tq:(g + 1) * tq], NEG)
            e = jnp.exp2(sg - jnp.max(sg, axis=0, keepdims=True))
            ps.append(e / jnp.sum(e, axis=0, keepdims=True) * c_valid_f)
        res["o_cmp"] = _dot(res["kvc"].T.astype(BF16), jnp.concatenate(ps, axis=1).astype(BF16))
        p_sum = ps[0]
        for g in range(1, G):
            p_sum = p_sum + ps[g]
        jb0 = lax.broadcasted_iota(jnp.int32, (LANES, LANES), 0) * SEL_BLOCK
        cb = lax.broadcasted_iota(jnp.int32, (LANES, LANES), 1) * CMP_STRIDE
        overlap = jnp.where((cb < jb0 + SEL_BLOCK) & (cb + CMP_BLOCK > jb0), 1.0, 0.0)
        overlap = overlap.astype(BF16)
        p_hi = p_sum.astype(BF16)
        p_lo = (p_sum - p_hi.astype(F32)).astype(BF16)
        res["imp"] = (_dot(overlap, p_hi) + _dot(overlap, p_lo))[:n_sel]

    def select():
        jrow = lax.broadcasted_iota(jnp.int32, (n_sel, tq), 0)
        tb = jnp.right_shift(qi * tq + lax.broadcasted_iota(jnp.int32, (n_sel, tq), 1),
                             SEL_SHIFT)
        forced = (jrow == 0) | (jrow == tb) | (jrow == tb - 1)
        score = jnp.where(forced, FORCE_SCORE, jnp.where(jrow <= tb, res["imp"], -1.0))
        rank = jnp.zeros((n_sel, tq), F32)
        for j2 in range(n_sel):
            other = score[j2:j2 + 1, :]
            beats = (other > score) | ((other == score) & (jrow > j2))
            rank = rank + jnp.where(beats, 1.0, 0.0)
        sel_bias = jnp.where(rank < min(SEL_TOPN, n_sel), 0.0, NEG)
        zrest = jnp.zeros((LANES - HEAD_DIM - n_sel, tq), F32)
        res["qst_sel"] = jnp.concatenate(
            [jnp.concatenate([xr, sel_bias, zrest], axis=0) for xr in rot], axis=1).astype(BF16)

    chain = [cmp_softmax, select]

    cmp_scores()
    fl = _Flash(qst_win, kw_scr, kwt_scr, *flash_scr)
    win_chunks = (tq + WIN - 1 + tkv - 1) // tkv
    first = diag0 + n_diag - win_chunks
    m = fl.m0
    fl.scores(first, 0)
    for c in range(win_chunks):
        if c + 1 < win_chunks:
            fl.scores(first + c + 1, (c + 1) % 2)
        if c < len(chain):
            chain[c]()
        m, a = fl.softmax(c % 2, m, band_bias(first + c))
        fl.values(first + c, c % 2, a)
    for stage in chain[win_chunks:]:
        stage()
    o_win = fl.result()
    o_cmp, qst_sel = res["o_cmp"], res["qst_sel"]

    fl = _Flash(qst_sel, ks_scr, kst_scr, *flash_scr)
    p_scr[1] = jnp.zeros((tkv, G * tq), BF16)
    fl.scores(0, 0)

    def pair(t, carry):
        m, a_prev = carry
        j0 = 2 * t
        fl.scores(j0 + 1, 1)
        fl.values(j0 - 1, 1, a_prev)
        m, a0 = fl.softmax(0, m)
        fl.scores(j0 + 2, 0)
        fl.values(j0, 0, a0)
        return fl.softmax(1, m)

    m, a_prev = lax.fori_loop(0, diag0 // 2, pair, (fl.m0, fl.a0))
    j0 = 2 * (diag0 // 2)
    fl.scores(j0 + 1, 1)
    fl.values(j0 - 1, 1, a_prev)
    m, a0 = fl.softmax(0, m, causal_bias(j0))
    fl.values(j0, 0, a0)
    m, a1 = fl.softmax(1, m, causal_bias(j0 + 1))
    fl.values(j0 + 1, 1, a1)
    o_sel = fl.result()

    heads = []
    for g in range(G):
        sl = slice(g * tq, (g + 1) * tq)
        gate = lambda i: gt_ref[0, pl.ds(i * NSA_HEADS + hk * G + g, 1), :]
        heads.append(gate(0) * o_cmp[HEAD_DIM:, sl] + gate(1) * o_sel[g] + gate(2) * o_win[g])
    o_ref[0] = jnp.concatenate(heads, axis=0).T.astype(o_ref.dtype)


def _nsa_attention(qt, cos_t, sin_t, tabs, kvcmp, kvs, kvw, gt, tq=256, tkv=256):
    B, _, S = qt.shape
    HK = NSA_KV_HEADS
    gw = NSA_GROUP * HEAD_DIM
    tab_q = pl.BlockSpec((1, ROPE_HALF, tq), lambda b, h, i: (b, 0, i))
    tab_k = pl.BlockSpec((1, S, LANES), lambda b, h, i: (b, 0, 0))
    kv_spec = pl.BlockSpec((1, 1, S, LANES), lambda b, h, i: (b, h, 0, 0))
    kv_scr = pltpu.VMEM((S, LANES), BF16)
    kvt_scr = pltpu.VMEM((S // tkv, LANES, tkv), BF16)
    return pl.pallas_call(
        functools.partial(_nsa_attn_kernel, tq=tq, tkv=tkv),
        grid=(B, HK, S // tq),
        in_specs=[
            pl.BlockSpec((1, gw, tq), lambda b, h, i: (b, h, i)),
            tab_q, tab_q, tab_k, tab_k, tab_k,
            pl.BlockSpec((1, 1, kvcmp.shape[2], LANES), lambda b, h, i: (b, h, 0, 0)),
            kv_spec, kv_spec,
            pl.BlockSpec((1, LANES, tq), lambda b, h, i: (b, 0, i)),
        ],
        out_specs=pl.BlockSpec((1, tq, gw), lambda b, h, i: (b, i, h)),
        out_shape=jax.ShapeDtypeStruct((B, S, NSA_Q), BF16),
        scratch_shapes=[kv_scr, kvt_scr, kv_scr, kvt_scr,
                        pltpu.VMEM((2, tkv, NSA_GROUP * tq), F32),
                        pltpu.VMEM((2, tkv, NSA_GROUP * tq), BF16),
                        pltpu.VMEM((SUBLANES + HEAD_DIM, NSA_GROUP * tq), F32)],
        compiler_params=_cparams(("parallel", "parallel", "arbitrary")),
        name="nsa_attention",
    )(qt, cos_t, sin_t, *tabs, kvcmp, kvs, kvw, gt)


def _dil_combine(o_refs, l_refs, scratch):
    for ref, scr in zip((*o_refs, *l_refs), scratch):
        dil, rows = ref.shape[1], ref.shape[2]
        for r in range(dil):
            rows_r = slice(None) if dil == 1 else pl.ds(r, rows, stride=dil)
            for ct in range(ref.shape[3] // LANES):
                scr[ct, rows_r, :] = ref[0, r, :, ct * LANES:(ct + 1) * LANES].astype(F32)
    outs, lses = scratch[:3], [s[0] for s in scratch[3:]]
    mx = jnp.maximum(jnp.maximum(lses[0], lses[1]), lses[2])
    ex = [jnp.exp(l - mx) for l in lses]
    den = ex[0] + ex[1] + ex[2]
    alphas = [e / den for e in ex]
    D = o_refs[0].shape[3]
    col_head = jnp.right_shift(lax.broadcasted_iota(jnp.int32, (LANES, D), 1), HEAD_SHIFT)
    spread = jnp.where(col_head == lax.broadcasted_iota(jnp.int32, (LANES, D), 0),
                       1.0, 0.0).astype(BF16)
    wide = []
    for a in alphas:
        hi = a.astype(BF16)
        wide.append(_dot(hi, spread) + _dot((a - hi.astype(F32)).astype(BF16), spread))
    slabs = []
    for c in range(D // LANES):
        cols = slice(c * LANES, (c + 1) * LANES)
        acc = wide[0][:, cols] * outs[0][c]
        for n in range(1, 3):
            acc = acc + wide[n][:, cols] * outs[n][c]
        slabs.append(acc.astype(BF16))
    return jnp.concatenate(slabs, axis=1)


def _dil_mix_ffn_kernel(o0_ref, o1_ref, o2_ref, l0_ref, l1_ref, l2_ref, wo_ref, gmix_ref, h_ref,
                        gpre_ref, wgu_ref, wd_ref, gpost_ref, y_ref, *scratch):
    o = _dil_combine((o0_ref, o1_ref, o2_ref), (l0_ref, l1_ref, l2_ref), scratch)
    h = h_ref[0] + _rms(_dot(o, wo_ref[...]), gmix_ref[...])
    y_ref[0] = _ffn_block(h, gpre_ref, wgu_ref, wd_ref, gpost_ref)


def _dil_mix_ffn(outs, lses, w_o, g_mix, h, g_pre, w_gu, w_down, g_post, tm=512):
    B, S, D = h.shape
    row = pl.BlockSpec((1, tm, D), lambda b, i: (b, i, 0))
    vec = pl.BlockSpec((1, D), lambda b, i: (0, 0))
    once = pl.Buffered(1)
    cls = lambda x: pl.BlockSpec((1, x.shape[1], tm // x.shape[1], x.shape[3]),
                                 lambda b, i: (b, 0, i, 0))
    return pl.pallas_call(
        _dil_mix_ffn_kernel,
        grid=(B, S // tm),
        in_specs=[cls(x) for x in (*outs, *lses)] + [
            pl.BlockSpec((D, D), lambda b, i: (0, 0), pipeline_mode=once), vec, row, vec,
            pl.BlockSpec((D, 2 * D_FF), lambda b, i: (0, 0), pipeline_mode=once),
            pl.BlockSpec((D_FF, D), lambda b, i: (0, 0), pipeline_mode=once),
            vec],
        out_specs=row,
        out_shape=jax.ShapeDtypeStruct((B, S, D), F32),
        scratch_shapes=([pltpu.VMEM((D // LANES, tm, LANES), F32)] * 3
                        + [pltpu.VMEM((1, tm, LANES), F32)] * 3),
        compiler_params=_cparams(("parallel", "parallel")),
        name="dil_out_swiglu_ffn",
    )(*outs, *lses, w_o, g_mix, h, g_pre, w_gu, w_down, g_post)


def _class_major(ref, dil):
    rows = ref.shape[0] // dil
    return jnp.concatenate([ref[pl.ds(r, rows, stride=dil), :] for r in range(dil)], axis=0)


def _dil_proj_kernel(h_ref, c_ref, s1_ref, s2_ref, g_ref, w_ref, q_ref, k_ref, v_ref, x_scr,
                     *, dil):
    D = D_MODEL
    rows = h_ref.shape[1] // dil
    if dil == 1:
        x = h_ref[0]
        c, s1, s2 = c_ref[0], s1_ref[0], s2_ref[0]
    else:
        for ct in range(D // LANES):
            x_scr[ct] = h_ref[0, :, ct * LANES:(ct + 1) * LANES]
        x = jnp.concatenate([_class_major(x_scr.at[ct], dil) for ct in range(D // LANES)],
                            axis=1)
        c, s1, s2 = (_class_major(ref.at[0], dil) for ref in (c_ref, s1_ref, s2_ref))
    a = _rms(x, g_ref[...]).astype(BF16)
    w = w_ref[...]
    q = _rope_wide(_dot(a, w[:, :D]), c, s1, s2) * (SCALE * LOG2E)
    k = _rope_wide(_dot(a, w[:, D:2 * D]), c, s1, s2)
    v = _dot(a, w[:, 2 * D:])
    for r in range(dil):
        sl = slice(r * rows, (r + 1) * rows)
        q_ref[0, r] = q[sl].astype(BF16)
        k_ref[0, r] = k[sl].astype(BF16)
        v_ref[0, r] = v[sl].astype(BF16)


def _dil_proj(h, tabs, g_pre, w, dil, tm=512):
    B, S, D = h.shape
    out_shape = jax.ShapeDtypeStruct((B, dil, S // dil, D), BF16)
    out_spec = pl.BlockSpec((1, dil, tm // dil, D), lambda b, i: (b, 0, i, 0))
    tab_spec = pl.BlockSpec((1, tm, LANES), lambda b, i: (b, i, 0))
    return pl.pallas_call(
        functools.partial(_dil_proj_kernel, dil=dil),
        grid=(B, S // tm),
        in_specs=[
            pl.BlockSpec((1, tm, D), lambda b, i: (b, i, 0)),
            tab_spec, tab_spec, tab_spec,
            pl.BlockSpec((1, D), lambda b, i: (0, 0)),
            pl.BlockSpec((D, 3 * D), lambda b, i: (0, 0)),
        ],
        out_specs=[out_spec, out_spec, out_spec],
        out_shape=[out_shape, out_shape, out_shape],
        scratch_shapes=[pltpu.VMEM((D // LANES, tm, LANES), F32)],
        compiler_params=_cparams(("parallel", "parallel")),
        name=f"dil_proj_d{dil}",
    )(h, *tabs, g_pre, w)


DIL_SUB = 2


def _dil_attn_kernel(q_ref, kp_ref, kc_ref, vp_ref, vc_ref, o_ref, lse_ref, *, band, has_prev):
    tq = DIL_BAND_BLOCK
    i = pl.program_id(2)
    nk = 2 * tq if has_prev else tq
    key = lax.broadcasted_iota(jnp.int32, (nk, tq), 0)
    rel = lax.broadcasted_iota(jnp.int32, (nk, tq), 1) + (nk - tq) - key
    mask = jnp.where(rel >= 0, jnp.where(rel <= band, 0.0, NEG), NEG)
    tr = lambda x: x.T
    zpad = jnp.zeros((HEAD_DIM, tq), BF16)

    subs = []
    if has_prev:
        qt = tr(q_ref[0, 0])
        k_all = jnp.concatenate([kp_ref[0, 0], kc_ref[0, 0]], axis=0)
        vt_all = jnp.concatenate([tr(vp_ref[0, 0]), tr(vc_ref[0, 0])], axis=1)
        before = jnp.where(i > 0, 0.0, NEG)
        first = jnp.where(key >= tq, mask, mask + before)
        for u in range(DIL_SUB):
            rows = slice(u * tq, (u + 1) * tq)
            subs.append((qt[:, rows], k_all[u * tq:u * tq + nk], vt_all[:, u * tq:u * tq + nk],
                         first if u == 0 else mask, (0, rows)))
    else:
        for u in range(DIL_SUB):
            subs.append((tr(q_ref[0, u]), kc_ref[0, u], tr(vc_ref[0, u]), mask,
                         (u, slice(None))))

    scores = []
    for qt_u, k_u, _, _, _ in subs:
        for h in range(DIL_HEADS):
            qh = qt_u[h * HEAD_DIM:(h + 1) * HEAD_DIM]
            pair = jnp.concatenate([qh, zpad] if h % 2 == 0 else [zpad, qh], axis=0)
            scores.append(_dot(k_u[:, (h // 2) * LANES:(h // 2 + 1) * LANES], pair))
    probs, maxes = [], []
    for n, s in enumerate(scores):
        s = s + subs[n // DIL_HEADS][3]
        m = jnp.max(s, axis=0, keepdims=True)
        probs.append(jnp.exp2(s - m).astype(BF16))
        maxes.append(m)
    ones = jnp.ones((SUBLANES, nk), BF16)
    for u, (_, _, vt_u, _, (lead, rows)) in enumerate(subs):
        outs, lse = [], []
        for h in range(DIL_HEADS):
            vt_h = jnp.concatenate([ones, vt_u[h * HEAD_DIM:(h + 1) * HEAD_DIM]], axis=0)
            r = _dot(vt_h, probs[u * DIL_HEADS + h])
            l = r[:1]
            outs.append(r[SUBLANES:] / l)
            lse.append(maxes[u * DIL_HEADS + h] * LN2 + jnp.log(l))
        o_ref[0, lead, rows, :] = jnp.concatenate(outs, axis=0).astype(o_ref.dtype).T
        lse.append(jnp.zeros((LANES - DIL_HEADS, tq), F32))
        lse_ref[0, lead, rows, :] = jnp.concatenate(lse, axis=0).T


def _dil_attention(q, k, v, band):
    B, dil, L, D = q.shape
    tq = DIL_BAND_BLOCK
    has_prev = L > tq
    if has_prev:
        grid = (B, dil, L // (DIL_SUB * tq))
        shape = lambda w: (1, 1, DIL_SUB * tq, w)
        cur = lambda w: pl.BlockSpec(shape(w), lambda b, r, i: (b, r, i, 0))
    else:
        grid = (B, dil // DIL_SUB, 1)
        shape = lambda w: (1, DIL_SUB, tq, w)
        cur = lambda w: pl.BlockSpec(shape(w), lambda b, r, i: (b, r, 0, 0))
    prev = pl.BlockSpec((1, 1, tq, D), lambda b, r, i: (b, r, jnp.maximum(DIL_SUB * i - 1, 0), 0))
    return pl.pallas_call(
        functools.partial(_dil_attn_kernel, band=band, has_prev=has_prev),
        grid=grid,
        in_specs=[cur(D), prev, cur(D), prev, cur(D)],
        out_specs=[cur(D), cur(LANES)],
        out_shape=[jax.ShapeDtypeStruct((B, dil, L, D), BF16),
                   jax.ShapeDtypeStruct((B, dil, L, LANES), F32)],
        compiler_params=_cparams(("parallel", "parallel", "arbitrary")),
        name=f"dil_attn_d{dil}",
    )(q, k, k, v, v)


FFN_CHUNK = 256


def _ffn_block(h, gpre_ref, wgu_ref, wd_ref, gpost_ref):
    a = _rms(h, gpre_ref[...]).astype(BF16)
    acc = None
    for j in range(D_FF // FFN_CHUNK):
        cols = slice(j * FFN_CHUNK, (j + 1) * FFN_CHUNK)
        ucols = slice(D_FF + j * FFN_CHUNK, D_FF + (j + 1) * FFN_CHUNK)
        g = _dot(a, wgu_ref[:, cols])
        u = _dot(a, wgu_ref[:, ucols])
        mid = (g * _sigmoid(g) * u).astype(BF16)
        part = _dot(mid, wd_ref[cols, :])
        acc = part if acc is None else acc + part
    return h + _rms(acc, gpost_ref[...])


def _mix_ffn_kernel(o_ref, wo_ref, gmix_ref, h_ref, gpre_ref, wgu_ref, wd_ref, gpost_ref, y_ref):
    h = h_ref[0] + _rms(_dot(o_ref[0], wo_ref[...]), gmix_ref[...])
    y_ref[0] = _ffn_block(h, gpre_ref, wgu_ref, wd_ref, gpost_ref)


def _mix_ffn(o, w_o, g_mix, h, g_pre, w_gu, w_down, g_post, tm=512):
    B, S, D = h.shape
    row = pl.BlockSpec((1, tm, D), lambda b, i: (b, i, 0))
    vec = pl.BlockSpec((1, D), lambda b, i: (0, 0))
    once = pl.Buffered(1)
    return pl.pallas_call(
        _mix_ffn_kernel,
        grid=(B, S // tm),
        in_specs=[row, pl.BlockSpec((D, D), lambda b, i: (0, 0), pipeline_mode=once), vec,
                  row, vec,
                  pl.BlockSpec((D, 2 * D_FF), lambda b, i: (0, 0), pipeline_mode=once),
                  pl.BlockSpec((D_FF, D), lambda b, i: (0, 0), pipeline_mode=once),
                  vec],
        out_specs=row,
        out_shape=jax.ShapeDtypeStruct((B, S, D), F32),
        compiler_params=_cparams(("parallel", "parallel")),
        name="mixer_out_swiglu_ffn",
    )(o, w_o, g_mix, h, g_pre, w_gu, w_down, g_post)


def _nsa_mixer(h, rope, g_pre, w_in, cmp_pos, wk1, wk2, wv1, wv2):
    tabs, cos_t, sin_t = rope
    qt, kvc, kvs, kvw, gt = _nsa_proj(h, g_pre, *_nsa_in_weight(w_in))
    kvcmp = _compress(kvc, *_compress_weights(cmp_pos, wk1, wk2, wv1, wv2))
    return _nsa_attention(qt, cos_t, sin_t, tabs, kvcmp, kvs, kvw, gt)


def _dil_mixer(h, tabs, g_pre, w_in):
    D = D_MODEL
    outs, lses = [], []
    for gi, (window, dil) in enumerate(DIL_PATTERNS):
        w = w_in[:, gi * 3 * D:(gi + 1) * 3 * D].astype(BF16)
        q, k, v = _dil_proj(h, tabs, g_pre, w, dil)
        o, lse = _dil_attention(q, k, v, window // dil)
        outs.append(o)
        lses.append(lse)
    return outs, lses


def kernel(x, positions, norm_mix_pre, norm_mix_post, norm_ffn_pre, norm_ffn_post, ffn_w_gu,
           ffn_w_down, nsa_w_in, nsa_cmp_pos, nsa_cmp_wk1, nsa_cmp_wk2, nsa_cmp_wv1,
           nsa_cmp_wv2, nsa_w_o, dil_w_in, dil_w_o):
    depth = norm_mix_pre.shape[0]
    rope = _rope_tables(positions)
    tabs = rope[0]
    vec = lambda g: g.reshape(1, D_MODEL)
    h = x
    for i in range(depth):
        j = i // 2
        ffn = (vec(norm_ffn_pre[i]), ffn_w_gu[i].astype(BF16), ffn_w_down[i].astype(BF16),
               vec(norm_ffn_post[i]))
        if i % 2 == 0:
            o = _nsa_mixer(h, rope, vec(norm_mix_pre[i]), nsa_w_in[j], nsa_cmp_pos[j],
                           nsa_cmp_wk1[j], nsa_cmp_wk2[j], nsa_cmp_wv1[j], nsa_cmp_wv2[j])
            h = _mix_ffn(o, nsa_w_o[j].astype(BF16), vec(norm_mix_post[i]), h, *ffn)
        else:
            outs, lses = _dil_mixer(h, tabs, vec(norm_mix_pre[i]), dil_w_in[j])
            h = _dil_mix_ffn(outs, lses, dil_w_o[j].astype(BF16), vec(norm_mix_post[i]), h, *ffn)
    return h
```

```python
import functools

import jax
import jax.numpy as jnp
from jax import lax
from jax.experimental import pallas as pl
from jax.experimental.pallas import tpu as pltpu

F32 = jnp.float32
BF16 = jnp.bfloat16

D_MODEL = 1024
HEAD_DIM = 64
HEAD_SHIFT = HEAD_DIM.bit_length() - 1
ROPE_DIM = HEAD_DIM // 4
ROPE_HALF = ROPE_DIM // 2
ROPE_THETA = 500000.0
NORM_EPS = 1e-6
NEG = -1e30
SCALE = HEAD_DIM ** -0.5
LOG2E = 1.4426950408889634
LN2 = 0.6931471805599453

NSA_HEADS = 16
NSA_KV_HEADS = 4
NSA_GROUP = NSA_HEADS // NSA_KV_HEADS
CMP_BLOCK = 32
CMP_STRIDE = 16
CMP_HIDDEN = 4 * HEAD_DIM
SEL_BLOCK = 64
SEL_SHIFT = SEL_BLOCK.bit_length() - 1
SEL_TOPN = 8
FORCE_SCORE = 1e4
WIN = 512
NSA_Q = NSA_HEADS * HEAD_DIM
NSA_KV = NSA_KV_HEADS * 2 * HEAD_DIM
NSA_GATES = 3 * NSA_HEADS

DIL_PATTERNS = ((128, 1), (512, 4), (2048, 16))
DIL_HEADS = 16
DIL_BAND_BLOCK = 128

D_FF = 2816

LANES = 128
SUBLANES = 8
MXU_TILE = 256
VMEM_LIMIT = 56 * 1024 * 1024

ROW_TILE = 512
NSA_Q_TILE = 256
NSA_KV_CHUNK = 256
DIL_SUB = 4
FFN_CHUNK = MXU_TILE


def _cparams(sem):
    return pltpu.CompilerParams(dimension_semantics=sem, vmem_limit_bytes=VMEM_LIMIT)


def _rms(x, g):
    ms = jnp.mean(x * x, axis=-1, keepdims=True)
    return x * lax.rsqrt(ms + NORM_EPS) * g


def _sigmoid(x):
    return 1.0 / (1.0 + jnp.exp(-x))


def _rope128(x, c, s1, s2):
    return x * c + pltpu.roll(x, ROPE_HALF, 1) * s1 + pltpu.roll(x, LANES - ROPE_HALF, 1) * s2


def _rope_wide(x, c, s1, s2):
    n = x.shape[1] // LANES
    return jnp.concatenate(
        [_rope128(x[:, i * LANES:(i + 1) * LANES], c, s1, s2) for i in range(n)], axis=1)


def _dot(a, b):
    return jnp.dot(a, b, preferred_element_type=F32)


def _dot_t(a, b):
    return lax.dot_general(a, b, (((1,), (1,)), ((), ())), preferred_element_type=F32)


def _rope_tables(positions):
    inv = ROPE_THETA ** (-jnp.arange(0, ROPE_DIM, 2, dtype=F32) / ROPE_DIM)
    pos = positions.astype(F32)
    d = jnp.arange(LANES) % HEAD_DIM
    ang = pos[..., None] * inv[d % ROPE_HALF]
    cos, sin = jnp.cos(ang), jnp.sin(ang)
    c = jnp.where(d < ROPE_DIM, cos, 1.0)
    s1 = jnp.where((d >= ROPE_HALF) & (d < ROPE_DIM), sin, 0.0)
    s2 = jnp.where(d < ROPE_HALF, -sin, 0.0)
    ang_t = pos[:, None, :] * inv[None, :, None]
    return (c, s1, s2), jnp.cos(ang_t), jnp.sin(ang_t)


def _nsa_proj_kernel(h_ref, g_ref, wt_ref, wkv_ref, qt_ref, kvc_ref, kvs_ref, kvw_ref, gt_ref):
    a = _rms(h_ref[0], g_ref[...]).astype(BF16)
    rt = _dot_t(wt_ref[...], a)
    qt_ref[0] = (rt[:NSA_Q] * (SCALE * LOG2E)).astype(BF16)
    gt_ref[0] = _sigmoid(rt[NSA_Q:])
    r = _dot(a, wkv_ref[...])
    for i, ref in enumerate((kvc_ref, kvs_ref, kvw_ref)):
        for hk in range(NSA_KV_HEADS):
            lo = i * NSA_KV + hk * LANES
            ref[0, hk] = r[:, lo:lo + LANES].astype(ref.dtype)


def _nsa_proj(h, g_pre, wt, wkv, tm=ROW_TILE):
    B, S, D = h.shape
    kv_shape = jax.ShapeDtypeStruct((B, NSA_KV_HEADS, S, LANES), BF16)
    kv_spec = pl.BlockSpec((1, NSA_KV_HEADS, tm, LANES), lambda b, i: (b, 0, i, 0))
    return pl.pallas_call(
        _nsa_proj_kernel,
        grid=(B, S // tm),
        in_specs=[
            pl.BlockSpec((1, tm, D), lambda b, i: (b, i, 0)),
            pl.BlockSpec((1, D), lambda b, i: (0, 0)),
            pl.BlockSpec((NSA_Q + LANES, D), lambda b, i: (0, 0)),
            pl.BlockSpec((D, 3 * NSA_KV), lambda b, i: (0, 0)),
        ],
        out_specs=[
            pl.BlockSpec((1, NSA_Q, tm), lambda b, i: (b, 0, i)),
            kv_spec, kv_spec, kv_spec,
            pl.BlockSpec((1, LANES, tm), lambda b, i: (b, 0, i)),
        ],
        out_shape=[
            jax.ShapeDtypeStruct((B, NSA_Q, S), BF16),
            jax.ShapeDtypeStruct(kv_shape.shape, F32),
            kv_shape, kv_shape,
            jax.ShapeDtypeStruct((B, LANES, S), F32),
        ],
        compiler_params=_cparams(("parallel", "parallel")),
        name="nsa_proj",
    )(h, g_pre, wt, wkv)


def _nsa_in_weight(w_in):
    dq, dkv = NSA_Q, NSA_KV_HEADS * HEAD_DIM
    kv = w_in[:, dq:dq + 6 * dkv].reshape(D_MODEL, 3, 2, NSA_KV_HEADS, HEAD_DIM)
    kv = kv.transpose(0, 1, 3, 2, 4).reshape(D_MODEL, 3 * NSA_KV)
    gates = jnp.pad(w_in[:, dq + 6 * dkv:], ((0, 0), (0, LANES - NSA_GATES)))
    wt = jnp.concatenate([w_in[:, :dq], gates], axis=1).T
    return wt.astype(BF16), kv.astype(BF16)


def _compress_kernel(x_ref, pos_ref, wk1_ref, wv1_ref, wk2_ref, wv2_ref, o_ref):
    n = x_ref.shape[2] // CMP_STRIDE
    zrows = jnp.zeros((HEAD_DIM, CMP_HIDDEN), BF16)
    hid = {}
    for r in range(CMP_STRIDE):
        x = x_ref[0, 0, pl.ds(r, n, stride=CMP_STRIDE), :]
        for half in range(2):
            l = half * CMP_STRIDE + r
            xb = (x + pos_ref[l:l + 1, :]).astype(BF16)
            rows = slice(l * HEAD_DIM, (l + 1) * HEAD_DIM)
            for name, w_ref, stack in (("k", wk1_ref, lambda w: [w, zrows]),
                                       ("v", wv1_ref, lambda w: [zrows, w])):
                part = _dot(xb, jnp.concatenate(stack(w_ref[rows, :]), axis=0))
                hid[name, half] = part if r == 0 else hid[name, half] + part
    out = None
    for name, w2_ref in (("k", wk2_ref), ("v", wv2_ref)):
        h = hid[name, 0] + pltpu.roll(hid[name, 1], n - 1, 0)
        part = _dot((h * _sigmoid(h)).astype(BF16), w2_ref[...])
        out = part if out is None else out + part
    o_ref[0, 0] = out


def _compress(kvc, pos2, wk1, wv1, wk2p, wv2p):
    B, HK, S, _ = kvc.shape
    nchunk = S // CMP_STRIDE
    full = lambda x: pl.BlockSpec(x.shape, lambda b, h: (0,) * x.ndim)
    return pl.pallas_call(
        _compress_kernel,
        grid=(B, HK),
        in_specs=[pl.BlockSpec((1, 1, S, LANES), lambda b, h: (b, h, 0, 0)),
                  full(pos2), full(wk1), full(wv1), full(wk2p), full(wv2p)],
        out_specs=pl.BlockSpec((1, 1, nchunk, LANES), lambda b, h: (b, h, 0, 0)),
        out_shape=jax.ShapeDtypeStruct((B, HK, nchunk, LANES), F32),
        compiler_params=_cparams(("parallel", "parallel")),
        name="nsa_compress",
    )(kvc, pos2, wk1, wv1, wk2p, wv2p)


def _compress_weights(cmp_pos, wk1, wk2, wv1, wv2):
    pos2 = jnp.concatenate([cmp_pos, cmp_pos], axis=1)
    wk2p = jnp.pad(wk2, ((0, 0), (0, HEAD_DIM)))
    wv2p = jnp.pad(wv2, ((0, 0), (HEAD_DIM, 0)))
    return pos2, wk1.astype(BF16), wv1.astype(BF16), wk2p.astype(BF16), wv2p.astype(BF16)


class _Flash:
    def __init__(self, qst, k_scr, kt_scr, tq, tkv, s_scr, p_scr, acc_scr):
        self.qst, self.k_scr, self.kt_scr = qst, k_scr, kt_scr
        self.tq, self.tkv = tq, tkv
        self.s_scr, self.p_scr, self.acc_scr = s_scr, p_scr, acc_scr
        self.last = kt_scr.shape[0] - 1
        width = NSA_GROUP * tq
        self.m0 = jnp.full((1, width), NEG, F32)
        self.a0 = jnp.ones((1, width), F32)
        acc_scr[...] = jnp.zeros_like(acc_scr)

    def scores(self, j, slot):
        off = pl.multiple_of(jnp.clip(j, 0, self.last) * self.tkv, self.tkv)
        self.s_scr[slot] = _dot(self.k_scr[pl.ds(off, self.tkv), :], self.qst)

    def values(self, j, slot, a):
        vt = self.kt_scr[jnp.clip(j, 0, self.last), HEAD_DIM - SUBLANES:, :]
        self.acc_scr[...] = a * self.acc_scr[...] + _dot(vt, self.p_scr[slot])

    def softmax(self, slot, m, bias=None):
        tq, ms = self.tq, []
        for g in range(NSA_GROUP):
            sl = slice(g * tq, (g + 1) * tq)
            sg = self.s_scr[slot, :, sl]
            if bias is not None:
                sg = sg + bias
            mg = jnp.maximum(m[:, sl], jnp.max(sg, axis=0, keepdims=True))
            self.p_scr[slot, :, sl] = jnp.exp2(sg - mg).astype(BF16)
            ms.append(mg)
        m_new = jnp.concatenate(ms, axis=1)
        return m_new, jnp.exp2(m - m_new)

    def result(self):
        acc = self.acc_scr[...]
        o = acc[SUBLANES:] / acc[SUBLANES - 1:SUBLANES]
        return [o[:, g * self.tq:(g + 1) * self.tq] for g in range(NSA_GROUP)]


def _nsa_attn_kernel(qt_ref, cos_ref, sin_ref, kc_ref, ks1_ref, ks2_ref, kvc_ref, kvs_ref,
                     kvw_ref, gt_ref, o_ref, ks_scr, kst_scr, kw_scr, kwt_scr, s_scr, p_scr,
                     acc_scr, *, tq, tkv):
    G = NSA_GROUP
    hk = pl.program_id(1)
    qi = pl.program_id(2)
    S = kvs_ref.shape[2]
    n_sel = S // SEL_BLOCK

    @pl.when(qi == 0)
    def _():
        is_key = lax.broadcasted_iota(jnp.int32, (S, LANES), 1) < HEAD_DIM
        c = jnp.where(is_key, kc_ref[0], 1.0)
        s1 = jnp.where(is_key, ks1_ref[0], 0.0)
        s2 = jnp.where(is_key, ks2_ref[0], 0.0)
        lane = lax.broadcasted_iota(jnp.int32, (S, LANES), 1)
        blk = jnp.right_shift(lax.broadcasted_iota(jnp.int32, (S, LANES), 0), SEL_SHIFT)
        onehot = jnp.where(lane - HEAD_DIM == blk, 1.0, 0.0)
        for src, scr, scr_t in ((kvs_ref, ks_scr, kst_scr), (kvw_ref, kw_scr, kwt_scr)):
            kr = _rope128(src[0, 0].astype(F32), c, s1, s2)
            scr[...] = (jnp.where(is_key, kr, onehot) if scr is ks_scr else kr).astype(BF16)
            ones_row = lax.broadcasted_iota(jnp.int32, (LANES, tkv), 0) == HEAD_DIM - 1
            for jj in range(S // tkv):
                krt = kr[jj * tkv:(jj + 1) * tkv].T
                scr_t[jj] = jnp.where(ones_row, 1.0, krt).astype(BF16)

    q = qt_ref[0].astype(F32)
    cos, sin = cos_ref[0], sin_ref[0]
    zpad = jnp.zeros((LANES - HEAD_DIM, tq), F32)
    plain, rot = [], []
    for g in range(G):
        x = q[g * HEAD_DIM:(g + 1) * HEAD_DIM]
        x1, x2 = x[:ROPE_HALF], x[ROPE_HALF:ROPE_DIM]
        xr = jnp.concatenate([x1 * cos - x2 * sin, x2 * cos + x1 * sin, x[ROPE_DIM:]], axis=0)
        plain.append(jnp.concatenate([x, zpad], axis=0))
        rot.append(xr)
    qst_plain = jnp.concatenate(plain, axis=1).astype(BF16)
    qst_win = jnp.concatenate([jnp.concatenate([xr, zpad], axis=0) for xr in rot],
                              axis=1).astype(BF16)

    key_r = lax.broadcasted_iota(jnp.int32, (tkv, tq), 0)
    qry = qi * tq + lax.broadcasted_iota(jnp.int32, (tkv, tq), 1)

    def causal_bias(j):
        return jnp.where(j * tkv + key_r <= qry, 0.0, NEG)

    def band_bias(j):
        d = qry - (j * tkv + key_r)
        inside = jnp.where(j >= 0, 0.0, NEG)
        return jnp.where(d >= 0, jnp.where(d <= WIN - 1, inside, NEG), NEG)

    flash_scr = (tq, tkv, s_scr, p_scr, acc_scr)
    n_diag = tq // tkv
    assert n_diag * tkv == tq and n_diag in (1, 2)
    diag0 = qi * n_diag

    res = {}

    def cmp_scores():
        res["kvc"] = kvc_ref[0, 0]
        res["s"] = _dot(res["kvc"].astype(BF16), qst_plain)

    def cmp_softmax():
        c_idx = lax.broadcasted_iota(jnp.int32, (LANES, tq), 0)
        t_pos = qi * tq + lax.broadcasted_iota(jnp.int32, (LANES, tq), 1)
        c_valid = (CMP_STRIDE * c_idx + CMP_BLOCK - 1) <= t_pos
        c_valid_f = jnp.where(c_valid, 1.0, 0.0)
        ps = []
        for g in range(G):
            sg = jnp.where(c_valid, res["s"][:, g * tq:(g + 1) * tq], NEG)
            e = jnp.exp2(sg - jnp.max(sg, axis=0, keepdims=True))
            ps.append(e / jnp.sum(e, axis=0, keepdims=True) * c_valid_f)
        res["o_cmp"] = _dot(res["kvc"].T.astype(BF16), jnp.concatenate(ps, axis=1).astype(BF16))
        p_sum = ps[0]
        for g in range(1, G):
            p_sum = p_sum + ps[g]
        jb0 = lax.broadcasted_iota(jnp.int32, (LANES, LANES), 0) * SEL_BLOCK
        cb = lax.broadcasted_iota(jnp.int32, (LANES, LANES), 1) * CMP_STRIDE
        overlap = jnp.where((cb < jb0 + SEL_BLOCK) & (cb + CMP_BLOCK > jb0), 1.0, 0.0)
        overlap = overlap.astype(BF16)
        p_hi = p_sum.astype(BF16)
        p_lo = (p_sum - p_hi.astype(F32)).astype(BF16)
        res["imp"] = (_dot(overlap, p_hi) + _dot(overlap, p_lo))[:n_sel]

    def select():
        jrow = lax.broadcasted_iota(jnp.int32, (n_sel, tq), 0)
        tb = jnp.right_shift(qi * tq + lax.broadcasted_iota(jnp.int32, (n_sel, tq), 1),
                             SEL_SHIFT)
        forced = (jrow == 0) | (jrow == tb) | (jrow == tb - 1)
        score = jnp.where(forced, FORCE_SCORE, jnp.where(jrow <= tb, res["imp"], -1.0))
        rank = jnp.zeros((n_sel, tq), F32)
        for j2 in range(n_sel):
            other = score[j2:j2 + 1, :]
            beats = (other > score) | ((other == score) & (jrow > j2))
            rank = rank + jnp.where(beats, 1.0, 0.0)
        sel_bias = jnp.where(rank < min(SEL_TOPN, n_sel), 0.0, NEG)
        zrest = jnp.zeros((LANES - HEAD_DIM - n_sel, tq), F32)
        res["qst_sel"] = jnp.concatenate(
            [jnp.concatenate([xr, sel_bias, zrest], axis=0) for xr in rot], axis=1).astype(BF16)

    chain = [cmp_softmax, select]

    cmp_scores()
    fl = _Flash(qst_win, kw_scr, kwt_scr, *flash_scr)
    win_chunks = (tq + WIN - 1 + tkv - 1) // tkv
    first = diag0 + n_diag - win_chunks
    m = fl.m0
    fl.scores(first, 0)
    for c in range(win_chunks):
        if c + 1 < win_chunks:
            fl.scores(first + c + 1, (c + 1) % 2)
        if c < len(chain):
            chain[c]()
        m, a = fl.softmax(c % 2, m, band_bias(first + c))
        fl.values(first + c, c % 2, a)
    for stage in chain[win_chunks:]:
        stage()
    o_win = fl.result()
    o_cmp, qst_sel = res["o_cmp"], res["qst_sel"]

    fl = _Flash(qst_sel, ks_scr, kst_scr, *flash_scr)
    p_scr[1] = jnp.zeros((tkv, G * tq), BF16)
    fl.scores(0, 0)

    def pair(t, carry):
        m, a_prev = carry
        j0 = 2 * t
        fl.scores(j0 + 1, 1)
        fl.values(j0 - 1, 1, a_prev)
        m, a0 = fl.softmax(0, m)
        fl.scores(j0 + 2, 0)
        fl.values(j0, 0, a0)
        return fl.softmax(1, m)

    m, a_prev = lax.fori_loop(0, diag0 // 2, pair, (fl.m0, fl.a0))
    j0 = 2 * (diag0 // 2)
    fl.scores(j0 + 1, 1)
    fl.values(j0 - 1, 1, a_prev)
    m, a0 = fl.softmax(0, m, causal_bias(j0))
    fl.values(j0, 0, a0)
    m, a1 = fl.softmax(1, m, causal_bias(j0 + 1))
    fl.values(j0 + 1, 1, a1)
    o_sel = fl.result()

    heads = []
    for g in range(G):
        sl = slice(g * tq, (g + 1) * tq)
        gate = lambda i: gt_ref[0, pl.ds(i * NSA_HEADS + hk * G + g, 1), :]
        heads.append(gate(0) * o_cmp[HEAD_DIM:, sl] + gate(1) * o_sel[g] + gate(2) * o_win[g])
    o_ref[0] = jnp.concatenate(heads, axis=0).T.astype(o_ref.dtype)


def _nsa_attention(qt, cos_t, sin_t, tabs, kvcmp, kvs, kvw, gt, tq=NSA_Q_TILE,
                   tkv=NSA_KV_CHUNK):
    B, _, S = qt.shape
    HK = NSA_KV_HEADS
    gw = NSA_GROUP * HEAD_DIM
    tab_q = pl.BlockSpec((1, ROPE_HALF, tq), lambda b, h, i: (b, 0, i))
    tab_k = pl.BlockSpec((1, S, LANES), lambda b, h, i: (b, 0, 0))
    kv_spec = pl.BlockSpec((1, 1, S, LANES), lambda b, h, i: (b, h, 0, 0))
    kv_scr = pltpu.VMEM((S, LANES), BF16)
    kvt_scr = pltpu.VMEM((S // tkv, LANES, tkv), BF16)
    return pl.pallas_call(
        functools.partial(_nsa_attn_kernel, tq=tq, tkv=tkv),
        grid=(B, HK, S // tq),
        in_specs=[
            pl.BlockSpec((1, gw, tq), lambda b, h, i: (b, h, i)),
            tab_q, tab_q, tab_k, tab_k, tab_k,
            pl.BlockSpec((1, 1, kvcmp.shape[2], LANES), lambda b, h, i: (b, h, 0, 0)),
            kv_spec, kv_spec,
            pl.BlockSpec((1, LANES, tq), lambda b, h, i: (b, 0, i)),
        ],
        out_specs=pl.BlockSpec((1, tq, gw), lambda b, h, i: (b, i, h)),
        out_shape=jax.ShapeDtypeStruct((B, S, NSA_Q), BF16),
        scratch_shapes=[kv_scr, kvt_scr, kv_scr, kvt_scr,
                        pltpu.VMEM((2, tkv, NSA_GROUP * tq), F32),
                        pltpu.VMEM((2, tkv, NSA_GROUP * tq), BF16),
                        pltpu.VMEM((SUBLANES + HEAD_DIM, NSA_GROUP * tq), F32)],
        compiler_params=_cparams(("parallel", "parallel", "arbitrary")),
        name="nsa_attention",
    )(qt, cos_t, sin_t, *tabs, kvcmp, kvs, kvw, gt)


def _dil_combine(o_refs, l_refs, scratch):
    for ref, scr in zip((*o_refs, *l_refs), scratch):
        dil, rows = ref.shape[1], ref.shape[2]
        for r in range(dil):
            rows_r = slice(None) if dil == 1 else pl.ds(r, rows, stride=dil)
            for ct in range(ref.shape[3] // LANES):
                scr[ct, rows_r, :] = ref[0, r, :, ct * LANES:(ct + 1) * LANES].astype(F32)
    outs, lses = scratch[:3], [s[0] for s in scratch[3:]]
    mx = jnp.maximum(jnp.maximum(lses[0], lses[1]), lses[2])
    ex = [jnp.exp(l - mx) for l in lses]
    den = ex[0] + ex[1] + ex[2]
    alphas = [e / den for e in ex]
    D = o_refs[0].shape[3]
    col_head = jnp.right_shift(lax.broadcasted_iota(jnp.int32, (LANES, D), 1), HEAD_SHIFT)
    spread = jnp.where(col_head == lax.broadcasted_iota(jnp.int32, (LANES, D), 0),
                       1.0, 0.0).astype(BF16)
    wide = []
    for a in alphas:
        hi = a.astype(BF16)
        wide.append(_dot(hi, spread) + _dot((a - hi.astype(F32)).astype(BF16), spread))
    slabs = []
    for c in range(D // LANES):
        cols = slice(c * LANES, (c + 1) * LANES)
        acc = wide[0][:, cols] * outs[0][c]
        for n in range(1, 3):
            acc = acc + wide[n][:, cols] * outs[n][c]
        slabs.append(acc.astype(BF16))
    return jnp.concatenate(slabs, axis=1)


def _dil_mix_ffn_kernel(o0_ref, o1_ref, o2_ref, l0_ref, l1_ref, l2_ref, wo_ref, gmix_ref, h_ref,
                        gpre_ref, wgu_ref, wd_ref, gpost_ref, y_ref, *scratch):
    o = _dil_combine((o0_ref, o1_ref, o2_ref), (l0_ref, l1_ref, l2_ref), scratch)
    h = h_ref[0] + _rms(_dot(o, wo_ref[...]), gmix_ref[...])
    y_ref[0] = _ffn_block(h, gpre_ref, wgu_ref, wd_ref, gpost_ref)


def _dil_mix_ffn(outs, lses, w_o, g_mix, h, g_pre, w_gu, w_down, g_post, tm=ROW_TILE):
    B, S, D = h.shape
    row = pl.BlockSpec((1, tm, D), lambda b, i: (b, i, 0))
    vec = pl.BlockSpec((1, D), lambda b, i: (0, 0))
    once = pl.Buffered(1)
    cls = lambda x: pl.BlockSpec((1, x.shape[1], tm // x.shape[1], x.shape[3]),
                                 lambda b, i: (b, 0, i, 0))
    return pl.pallas_call(
        _dil_mix_ffn_kernel,
        grid=(B, S // tm),
        in_specs=[cls(x) for x in (*outs, *lses)] + [
            pl.BlockSpec((D, D), lambda b, i: (0, 0), pipeline_mode=once), vec, row, vec,
            pl.BlockSpec((D, 2 * D_FF), lambda b, i: (0, 0), pipeline_mode=once),
            pl.BlockSpec((D_FF, D), lambda b, i: (0, 0), pipeline_mode=once),
            vec],
        out_specs=row,
        out_shape=jax.ShapeDtypeStruct((B, S, D), F32),
        scratch_shapes=([pltpu.VMEM((D // LANES, tm, LANES), F32)] * 3
                        + [pltpu.VMEM((1, tm, LANES), F32)] * 3),
        compiler_params=_cparams(("parallel", "parallel")),
        name="dil_out_swiglu_ffn",
    )(*outs, *lses, w_o, g_mix, h, g_pre, w_gu, w_down, g_post)


def _class_major(ref, dil):
    rows = ref.shape[0] // dil
    return jnp.concatenate([ref[pl.ds(r, rows, stride=dil), :] for r in range(dil)], axis=0)


def _dil_proj_kernel(h_ref, c_ref, s1_ref, s2_ref, g_ref, w_ref, q_ref, k_ref, v_ref, x_scr,
                     *, dil):
    D = D_MODEL
    rows = h_ref.shape[1] // dil
    if dil == 1:
        x = h_ref[0]
        c, s1, s2 = c_ref[0], s1_ref[0], s2_ref[0]
    else:
        for ct in range(D // LANES):
            x_scr[ct] = h_ref[0, :, ct * LANES:(ct + 1) * LANES]
        x = jnp.concatenate([_class_major(x_scr.at[ct], dil) for ct in range(D // LANES)],
                            axis=1)
        c, s1, s2 = (_class_major(ref.at[0], dil) for ref in (c_ref, s1_ref, s2_ref))
    a = _rms(x, g_ref[...]).astype(BF16)
    w = w_ref[...]
    q = _rope_wide(_dot(a, w[:, :D]), c, s1, s2) * (SCALE * LOG2E)
    k = _rope_wide(_dot(a, w[:, D:2 * D]), c, s1, s2)
    v = _dot(a, w[:, 2 * D:])
    for r in range(dil):
        sl = slice(r * rows, (r + 1) * rows)
        q_ref[0, r] = q[sl].astype(BF16)
        k_ref[0, r] = k[sl].astype(BF16)
        v_ref[0, r] = v[sl].astype(BF16)


def _dil_proj(h, tabs, g_pre, w, dil, tm=ROW_TILE):
    B, S, D = h.shape
    out_shape = jax.ShapeDtypeStruct((B, dil, S // dil, D), BF16)
    out_spec = pl.BlockSpec((1, dil, tm // dil, D), lambda b, i: (b, 0, i, 0))
    tab_spec = pl.BlockSpec((1, tm, LANES), lambda b, i: (b, i, 0))
    return pl.pallas_call(
        functools.partial(_dil_proj_kernel, dil=dil),
        grid=(B, S // tm),
        in_specs=[
            pl.BlockSpec((1, tm, D), lambda b, i: (b, i, 0)),
            tab_spec, tab_spec, tab_spec,
            pl.BlockSpec((1, D), lambda b, i: (0, 0)),
            pl.BlockSpec((D, 3 * D), lambda b, i: (0, 0)),
        ],
        out_specs=[out_spec, out_spec, out_spec],
        out_shape=[out_shape, out_shape, out_shape],
        scratch_shapes=[pltpu.VMEM((D // LANES, tm, LANES), F32)],
        compiler_params=_cparams(("parallel", "parallel")),
        name=f"dil_proj_d{dil}",
    )(h, *tabs, g_pre, w)


def _dil_attn_kernel(q_ref, kp_ref, kc_ref, vp_ref, vc_ref, o_ref, lse_ref, *, band, has_prev):
    tq = DIL_BAND_BLOCK
    i = pl.program_id(2)
    nk = 2 * tq if has_prev else tq
    key = lax.broadcasted_iota(jnp.int32, (nk, tq), 0)
    rel = lax.broadcasted_iota(jnp.int32, (nk, tq), 1) + (nk - tq) - key
    mask = jnp.where(rel >= 0, jnp.where(rel <= band, 0.0, NEG), NEG)
    tr = lambda x: x.T
    zpad = jnp.zeros((HEAD_DIM, tq), BF16)

    subs = []
    if has_prev:
        qt = tr(q_ref[0, 0])
        k_all = jnp.concatenate([kp_ref[0, 0], kc_ref[0, 0]], axis=0)
        vt_all = jnp.concatenate([tr(vp_ref[0, 0]), tr(vc_ref[0, 0])], axis=1)
        before = jnp.where(i > 0, 0.0, NEG)
        first = jnp.where(key >= tq, mask, mask + before)
        for u in range(DIL_SUB):
            rows = slice(u * tq, (u + 1) * tq)
            subs.append((qt[:, rows], k_all[u * tq:u * tq + nk], vt_all[:, u * tq:u * tq + nk],
                         first if u == 0 else mask, (0, rows)))
    else:
        for u in range(DIL_SUB):
            subs.append((tr(q_ref[0, u]), kc_ref[0, u], tr(vc_ref[0, u]), mask,
                         (u, slice(None))))

    scores = []
    for qt_u, k_u, _, _, _ in subs:
        for h in range(DIL_HEADS):
            qh = qt_u[h * HEAD_DIM:(h + 1) * HEAD_DIM]
            pair = jnp.concatenate([qh, zpad] if h % 2 == 0 else [zpad, qh], axis=0)
            scores.append(_dot(k_u[:, (h // 2) * LANES:(h // 2 + 1) * LANES], pair))
    probs, maxes = [], []
    for n, s in enumerate(scores):
        s = s + subs[n // DIL_HEADS][3]
        m = jnp.max(s, axis=0, keepdims=True)
        probs.append(jnp.exp2(s - m).astype(BF16))
        maxes.append(m)
    ones = jnp.ones((SUBLANES, nk), BF16)
    for u, (_, _, vt_u, _, (lead, rows)) in enumerate(subs):
        outs, lse = [], []
        for h in range(DIL_HEADS):
            vt_h = jnp.concatenate([ones, vt_u[h * HEAD_DIM:(h + 1) * HEAD_DIM]], axis=0)
            r = _dot(vt_h, probs[u * DIL_HEADS + h])
            l = r[:1]
            outs.append(r[SUBLANES:] / l)
            lse.append(maxes[u * DIL_HEADS + h] * LN2 + jnp.log(l))
        o_ref[0, lead, rows, :] = jnp.concatenate(outs, axis=0).astype(o_ref.dtype).T
        lse.append(jnp.zeros((LANES - DIL_HEADS, tq), F32))
        lse_ref[0, lead, rows, :] = jnp.concatenate(lse, axis=0).T


def _dil_attention(q, k, v, band):
    B, dil, L, D = q.shape
    tq = DIL_BAND_BLOCK
    has_prev = L > tq
    if has_prev:
        grid = (B, dil, L // (DIL_SUB * tq))
        shape = lambda w: (1, 1, DIL_SUB * tq, w)
        cur = lambda w: pl.BlockSpec(shape(w), lambda b, r, i: (b, r, i, 0))
    else:
        grid = (B, dil // DIL_SUB, 1)
        shape = lambda w: (1, DIL_SUB, tq, w)
        cur = lambda w: pl.BlockSpec(shape(w), lambda b, r, i: (b, r, 0, 0))
    prev = pl.BlockSpec((1, 1, tq, D), lambda b, r, i: (b, r, jnp.maximum(DIL_SUB * i - 1, 0), 0))
    return pl.pallas_call(
        functools.partial(_dil_attn_kernel, band=band, has_prev=has_prev),
        grid=grid,
        in_specs=[cur(D), prev, cur(D), prev, cur(D)],
        out_specs=[cur(D), cur(LANES)],
        out_shape=[jax.ShapeDtypeStruct((B, dil, L, D), BF16),
                   jax.ShapeDtypeStruct((B, dil, L, LANES), F32)],
        compiler_params=_cparams(("parallel", "parallel", "arbitrary")),
        name=f"dil_attn_d{dil}",
    )(q, k, k, v, v)


def _ffn_block(h, gpre_ref, wgu_ref, wd_ref, gpost_ref):
    a = _rms(h, gpre_ref[...]).astype(BF16)
    acc = None
    for j in range(D_FF // FFN_CHUNK):
        cols = slice(j * FFN_CHUNK, (j + 1) * FFN_CHUNK)
        ucols = slice(D_FF + j * FFN_CHUNK, D_FF + (j + 1) * FFN_CHUNK)
        g = _dot(a, wgu_ref[:, cols])
        u = _dot(a, wgu_ref[:, ucols])
        mid = (g * _sigmoid(g) * u).astype(BF16)
        part = _dot(mid, wd_ref[cols, :])
        acc = part if acc is None else acc + part
    return h + _rms(acc, gpost_ref[...])


def _mix_ffn_kernel(o_ref, wo_ref, gmix_ref, h_ref, gpre_ref, wgu_ref, wd_ref, gpost_ref, y_ref):
    h = h_ref[0] + _rms(_dot(o_ref[0], wo_ref[...]), gmix_ref[...])
    y_ref[0] = _ffn_block(h, gpre_ref, wgu_ref, wd_ref, gpost_ref)


def _mix_ffn(o, w_o, g_mix, h, g_pre, w_gu, w_down, g_post, tm=ROW_TILE):
    B, S, D = h.shape
    row = pl.BlockSpec((1, tm, D), lambda b, i: (b, i, 0))
    vec = pl.BlockSpec((1, D), lambda b, i: (0, 0))
    once = pl.Buffered(1)
    return pl.pallas_call(
        _mix_ffn_kernel,
        grid=(B, S // tm),
        in_specs=[row, pl.BlockSpec((D, D), lambda b, i: (0, 0), pipeline_mode=once), vec,
                  row, vec,
                  pl.BlockSpec((D, 2 * D_FF), lambda b, i: (0, 0), pipeline_mode=once),
                  pl.BlockSpec((D_FF, D), lambda b, i: (0, 0), pipeline_mode=once),
                  vec],
        out_specs=row,
        out_shape=jax.ShapeDtypeStruct((B, S, D), F32),
        compiler_params=_cparams(("parallel", "parallel")),
        name="mixer_out_swiglu_ffn",
    )(o, w_o, g_mix, h, g_pre, w_gu, w_down, g_post)


def _nsa_mixer(h, rope, g_pre, w_in, cmp_pos, wk1, wk2, wv1, wv2):
    tabs, cos_t, sin_t = rope
    qt, kvc, kvs, kvw, gt = _nsa_proj(h, g_pre, *_nsa_in_weight(w_in))
    kvcmp = _compress(kvc, *_compress_weights(cmp_pos, wk1, wk2, wv1, wv2))
    return _nsa_attention(qt, cos_t, sin_t, tabs, kvcmp, kvs, kvw, gt)


def _dil_mixer(h, tabs, g_pre, w_in):
    D = D_MODEL
    outs, lses = [], []
    for gi, (window, dil) in enumerate(DIL_PATTERNS):
        w = w_in[:, gi * 3 * D:(gi + 1) * 3 * D].astype(BF16)
        q, k, v = _dil_proj(h, tabs, g_pre, w, dil)
        o, lse = _dil_attention(q, k, v, window // dil)
        outs.append(o)
        lses.append(lse)
    return outs, lses


def kernel(x, positions, norm_mix_pre, norm_mix_post, norm_ffn_pre, norm_ffn_post, ffn_w_gu,
           ffn_w_down, nsa_w_in, nsa_cmp_pos, nsa_cmp_wk1, nsa_cmp_wk2, nsa_cmp_wv1,
           nsa_cmp_wv2, nsa_w_o, dil_w_in, dil_w_o):
    depth = norm_mix_pre.shape[0]
    rope = _rope_tables(positions)
    tabs = rope[0]
    vec = lambda g: g.reshape(1, D_MODEL)
    h = x
    for i in range(depth):
        j = i // 2
        ffn = (vec(norm_ffn_pre[i]), ffn_w_gu[i].astype(BF16), ffn_w_down[i].astype(BF16),
               vec(norm_ffn_post[i]))
        if i % 2 == 0:
            o = _nsa_mixer(h, rope, vec(norm_mix_pre[i]), nsa_w_in[j], nsa_cmp_pos[j],
                           nsa_cmp_wk1[j], nsa_cmp_wk2[j], nsa_cmp_wv1[j], nsa_cmp_wv2[j])
            h = _mix_ffn(o, nsa_w_o[j].astype(BF16), vec(norm_mix_post[i]), h, *ffn)
        else:
            outs, lses = _dil_mixer(h, tabs, vec(norm_mix_pre[i]), dil_w_in[j])
            h = _dil_mix_ffn(outs, lses, dil_w_o[j].astype(BF16), vec(norm_mix_post[i]), h, *ffn)
    return h
```

```python
import functools

import jax
import jax.numpy as jnp
from jax import lax
from jax.experimental import pallas as pl
from jax.experimental.pallas import tpu as pltpu

F32 = jnp.float32
BF16 = jnp.bfloat16

D_MODEL = 1024
HEAD_DIM = 64
HEAD_SHIFT = HEAD_DIM.bit_length() - 1
ROPE_DIM = HEAD_DIM // 4
ROPE_HALF = ROPE_DIM // 2
ROPE_THETA = 500000.0
NORM_EPS = 1e-6
NEG = -1e30
SCALE = HEAD_DIM ** -0.5
LOG2E = 1.4426950408889634
LN2 = 0.6931471805599453

NSA_HEADS = 16
NSA_KV_HEADS = 4
NSA_GROUP = NSA_HEADS // NSA_KV_HEADS
CMP_BLOCK = 32
CMP_STRIDE = 16
CMP_HIDDEN = 4 * HEAD_DIM
SEL_BLOCK = 64
SEL_SHIFT = SEL_BLOCK.bit_length() - 1
SEL_TOPN = 8
FORCE_SCORE = 1e4
WIN = 512
NSA_Q = NSA_HEADS * HEAD_DIM
NSA_KV = NSA_KV_HEADS * 2 * HEAD_DIM
NSA_GATES = 3 * NSA_HEADS

DIL_PATTERNS = ((128, 1), (512, 4), (2048, 16))
DIL_HEADS = 16
DIL_BAND_BLOCK = 128

D_FF = 2816

LANES = 128
SUBLANES = 8
MXU_TILE = 256
VMEM_LIMIT = 56 * 1024 * 1024

ROW_TILE = 512
NSA_Q_TILE = 256
NSA_KV_CHUNK = 256
DIL_SUB = 4
FFN_CHUNK = MXU_TILE


def _cparams(sem):
    return pltpu.CompilerParams(dimension_semantics=sem, vmem_limit_bytes=VMEM_LIMIT)


def _rms(x, g):
    ms = jnp.mean(x * x, axis=-1, keepdims=True)
    return x * lax.rsqrt(ms + NORM_EPS) * g


def _sigmoid(x):
    return 1.0 / (1.0 + jnp.exp(-x))


def _rope128(x, c, s1, s2):
    return x * c + pltpu.roll(x, ROPE_HALF, 1) * s1 + pltpu.roll(x, LANES - ROPE_HALF, 1) * s2


def _rope_wide(x, c, s1, s2):
    n = x.shape[1] // LANES
    return jnp.concatenate(
        [_rope128(x[:, i * LANES:(i + 1) * LANES], c, s1, s2) for i in range(n)], axis=1)


def _dot(a, b):
    return jnp.dot(a, b, preferred_element_type=F32)


def _dot_t(a, b):
    return lax.dot_general(a, b, (((1,), (1,)), ((), ())), preferred_element_type=F32)


def _rope_tables(positions):
    inv = ROPE_THETA ** (-jnp.arange(0, ROPE_DIM, 2, dtype=F32) / ROPE_DIM)
    pos = positions.astype(F32)
    d = jnp.arange(LANES) % HEAD_DIM
    ang = pos[..., None] * inv[d % ROPE_HALF]
    cos, sin = jnp.cos(ang), jnp.sin(ang)
    c = jnp.where(d < ROPE_DIM, cos, 1.0)
    s1 = jnp.where((d >= ROPE_HALF) & (d < ROPE_DIM), sin, 0.0)
    s2 = jnp.where(d < ROPE_HALF, -sin, 0.0)
    ang_t = pos[:, None, :] * inv[None, :, None]
    return (c, s1, s2), jnp.cos(ang_t), jnp.sin(ang_t)


def _nsa_proj_kernel(h_ref, g_ref, wt_ref, wkv_ref, qt_ref, kvc_ref, kvs_ref, kvw_ref, gt_ref):
    a = _rms(h_ref[0], g_ref[...]).astype(BF16)
    rt = _dot_t(wt_ref[...], a)
    qt_ref[0] = (rt[:NSA_Q] * (SCALE * LOG2E)).astype(BF16)
    gt_ref[0] = _sigmoid(rt[NSA_Q:])
    r = _dot(a, wkv_ref[...])
    for i, ref in enumerate((kvc_ref, kvs_ref, kvw_ref)):
        for hk in range(NSA_KV_HEADS):
            lo = i * NSA_KV + hk * LANES
            ref[0, hk] = r[:, lo:lo + LANES].astype(ref.dtype)


def _nsa_proj(h, g_pre, wt, wkv, tm=ROW_TILE):
    B, S, D = h.shape
    kv_shape = jax.ShapeDtypeStruct((B, NSA_KV_HEADS, S, LANES), BF16)
    kv_spec = pl.BlockSpec((1, NSA_KV_HEADS, tm, LANES), lambda b, i: (b, 0, i, 0))
    return pl.pallas_call(
        _nsa_proj_kernel,
        grid=(B, S // tm),
        in_specs=[
            pl.BlockSpec((1, tm, D), lambda b, i: (b, i, 0)),
            pl.BlockSpec((1, D), lambda b, i: (0, 0)),
            pl.BlockSpec((NSA_Q + LANES, D), lambda b, i: (0, 0)),
            pl.BlockSpec((D, 3 * NSA_KV), lambda b, i: (0, 0)),
        ],
        out_specs=[
            pl.BlockSpec((1, NSA_Q, tm), lambda b, i: (b, 0, i)),
            kv_spec, kv_spec, kv_spec,
            pl.BlockSpec((1, LANES, tm), lambda b, i: (b, 0, i)),
        ],
        out_shape=[
            jax.ShapeDtypeStruct((B, NSA_Q, S), BF16),
            jax.ShapeDtypeStruct(kv_shape.shape, F32),
            kv_shape, kv_shape,
            jax.ShapeDtypeStruct((B, LANES, S), F32),
        ],
        compiler_params=_cparams(("parallel", "parallel")),
        name="nsa_proj",
    )(h, g_pre, wt, wkv)


def _nsa_in_weight(w_in):
    dq, dkv = NSA_Q, NSA_KV_HEADS * HEAD_DIM
    kv = w_in[:, dq:dq + 6 * dkv].reshape(D_MODEL, 3, 2, NSA_KV_HEADS, HEAD_DIM)
    kv = kv.transpose(0, 1, 3, 2, 4).reshape(D_MODEL, 3 * NSA_KV)
    gates = jnp.pad(w_in[:, dq + 6 * dkv:], ((0, 0), (0, LANES - NSA_GATES)))
    wt = jnp.concatenate([w_in[:, :dq], gates], axis=1).T
    return wt.astype(BF16), kv.astype(BF16)


def _compress_kernel(x_ref, pos_ref, wk1_ref, wv1_ref, wk2_ref, wv2_ref, o_ref):
    n = x_ref.shape[2] // CMP_STRIDE
    zrows = jnp.zeros((HEAD_DIM, CMP_HIDDEN), BF16)
    hid = {}
    for r in range(CMP_STRIDE):
        x = x_ref[0, 0, pl.ds(r, n, stride=CMP_STRIDE), :]
        for half in range(2):
            l = half * CMP_STRIDE + r
            xb = (x + pos_ref[l:l + 1, :]).astype(BF16)
            rows = slice(l * HEAD_DIM, (l + 1) * HEAD_DIM)
            for name, w_ref, stack in (("k", wk1_ref, lambda w: [w, zrows]),
                                       ("v", wv1_ref, lambda w: [zrows, w])):
                part = _dot(xb, jnp.concatenate(stack(w_ref[rows, :]), axis=0))
                hid[name, half] = part if r == 0 else hid[name, half] + part
    out = None
    for name, w2_ref in (("k", wk2_ref), ("v", wv2_ref)):
        h = hid[name, 0] + pltpu.roll(hid[name, 1], n - 1, 0)
        part = _dot((h * _sigmoid(h)).astype(BF16), w2_ref[...])
        out = part if out is None else out + part
    o_ref[0, 0] = out


def _compress(kvc, pos2, wk1, wv1, wk2p, wv2p):
    B, HK, S, _ = kvc.shape
    nchunk = S // CMP_STRIDE
    full = lambda x: pl.BlockSpec(x.shape, lambda b, h: (0,) * x.ndim)
    return pl.pallas_call(
        _compress_kernel,
        grid=(B, HK),
        in_specs=[pl.BlockSpec((1, 1, S, LANES), lambda b, h: (b, h, 0, 0)),
                  full(pos2), full(wk1), full(wv1), full(wk2p), full(wv2p)],
        out_specs=pl.BlockSpec((1, 1, nchunk, LANES), lambda b, h: (b, h, 0, 0)),
        out_shape=jax.ShapeDtypeStruct((B, HK, nchunk, LANES), F32),
        compiler_params=_cparams(("parallel", "parallel")),
        name="nsa_compress",
    )(kvc, pos2, wk1, wv1, wk2p, wv2p)


def _compress_weights(cmp_pos, wk1, wk2, wv1, wv2):
    pos2 = jnp.concatenate([cmp_pos, cmp_pos], axis=1)
    wk2p = jnp.pad(wk2, ((0, 0), (0, HEAD_DIM)))
    wv2p = jnp.pad(wv2, ((0, 0), (HEAD_DIM, 0)))
    return pos2, wk1.astype(BF16), wv1.astype(BF16), wk2p.astype(BF16), wv2p.astype(BF16)


class _Flash:
    def __init__(self, qst, k_scr, kt_scr, tq, tkv, s_scr, p_scr, acc_scr):
        self.qst, self.k_scr, self.kt_scr = qst, k_scr, kt_scr
        self.tq, self.tkv = tq, tkv
        self.s_scr, self.p_scr, self.acc_scr = s_scr, p_scr, acc_scr
        self.last = kt_scr.shape[0] - 1
        width = NSA_GROUP * tq
        self.m0 = jnp.full((1, width), NEG, F32)
        self.a0 = jnp.ones((1, width), F32)
        acc_scr[...] = jnp.zeros_like(acc_scr)

    def scores(self, j, slot):
        off = pl.multiple_of(jnp.clip(j, 0, self.last) * self.tkv, self.tkv)
        self.s_scr[slot] = _dot(self.k_scr[pl.ds(off, self.tkv), :], self.qst)

    def values(self, j, slot, a):
        vt = self.kt_scr[jnp.clip(j, 0, self.last), HEAD_DIM - SUBLANES:, :]
        self.acc_scr[...] = a * self.acc_scr[...] + _dot(vt, self.p_scr[slot])

    def softmax(self, slot, m, bias=None):
        tq, ms = self.tq, []
        for g in range(NSA_GROUP):
            sl = slice(g * tq, (g + 1) * tq)
            sg = self.s_scr[slot, :, sl]
            if bias is not None:
                sg = sg + bias
            mg = jnp.maximum(m[:, sl], jnp.max(sg, axis=0, keepdims=True))
            self.p_scr[slot, :, sl] = jnp.exp2(sg - mg).astype(BF16)
            ms.append(mg)
        m_new = jnp.concatenate(ms, axis=1)
        return m_new, jnp.exp2(m - m_new)

    def result(self):
        acc = self.acc_scr[...]
        o = acc[SUBLANES:] / acc[SUBLANES - 1:SUBLANES]
        return [o[:, g * self.tq:(g + 1) * self.tq] for g in range(NSA_GROUP)]


def _nsa_attn_kernel(qt_ref, cos_ref, sin_ref, kc_ref, ks1_ref, ks2_ref, kvc_ref, kvs_ref,
                     kvw_ref, gt_ref, o_ref, ks_scr, kst_scr, kw_scr, kwt_scr, s_scr, p_scr,
                     acc_scr, *, tq, tkv):
    G = NSA_GROUP
    hk = pl.program_id(1)
    qi = pl.program_id(2)
    S = kvs_ref.shape[2]
    n_sel = S // SEL_BLOCK

    @pl.when(qi == 0)
    def _():
        is_key = lax.broadcasted_iota(jnp.int32, (S, LANES), 1) < HEAD_DIM
        c = jnp.where(is_key, kc_ref[0], 1.0)
        s1 = jnp.where(is_key, ks1_ref[0], 0.0)
        s2 = jnp.where(is_key, ks2_ref[0], 0.0)
        lane = lax.broadcasted_iota(jnp.int32, (S, LANES), 1)
        blk = jnp.right_shift(lax.broadcasted_iota(jnp.int32, (S, LANES), 0), SEL_SHIFT)
        onehot = jnp.where(lane - HEAD_DIM == blk, 1.0, 0.0)
        for src, scr, scr_t in ((kvs_ref, ks_scr, kst_scr), (kvw_ref, kw_scr, kwt_scr)):
            kr = _rope128(src[0, 0].astype(F32), c, s1, s2)
            scr[...] = (jnp.where(is_key, kr, onehot) if scr is ks_scr else kr).astype(BF16)
            ones_row = lax.broadcasted_iota(jnp.int32, (LANES, tkv), 0) == HEAD_DIM - 1
            for jj in range(S // tkv):
                krt = kr[jj * tkv:(jj + 1) * tkv].T
                scr_t[jj] = jnp.where(ones_row, 1.0, krt).astype(BF16)

    q = qt_ref[0].astype(F32)
    cos, sin = cos_ref[0], sin_ref[0]
    zpad = jnp.zeros((LANES - HEAD_DIM, tq), F32)
    plain, rot = [], []
    for g in range(G):
        x = q[g * HEAD_DIM:(g + 1) * HEAD_DIM]
        x1, x2 = x[:ROPE_HALF], x[ROPE_HALF:ROPE_DIM]
        xr = jnp.concatenate([x1 * cos - x2 * sin, x2 * cos + x1 * sin, x[ROPE_DIM:]], axis=0)
        plain.append(jnp.concatenate([x, zpad], axis=0))
        rot.append(xr)
    qst_plain = jnp.concatenate(plain, axis=1).astype(BF16)
    qst_win = jnp.concatenate([jnp.concatenate([xr, zpad], axis=0) for xr in rot],
                              axis=1).astype(BF16)

    key_r = lax.broadcasted_iota(jnp.int32, (tkv, tq), 0)
    qry = qi * tq + lax.broadcasted_iota(jnp.int32, (tkv, tq), 1)

    def causal_bias(j):
        return jnp.where(j * tkv + key_r <= qry, 0.0, NEG)

    def band_bias(j):
        d = qry - (j * tkv + key_r)
        inside = jnp.where(j >= 0, 0.0, NEG)
        return jnp.where(d >= 0, jnp.where(d <= WIN - 1, inside, NEG), NEG)

    flash_scr = (tq, tkv, s_scr, p_scr, acc_scr)
    n_diag = tq // tkv
    assert n_diag * tkv == tq and n_diag in (1, 2)
    diag0 = qi * n_diag

    res = {}

    def cmp_scores():
        res["kvc"] = kvc_ref[0, 0]
        res["s"] = _dot(res["kvc"].astype(BF16), qst_plain)

    def cmp_softmax():
        c_idx = lax.broadcasted_iota(jnp.int32, (LANES, tq), 0)
        t_pos = qi * tq + lax.broadcasted_iota(jnp.int32, (LANES, tq), 1)
        c_valid = (CMP_STRIDE * c_idx + CMP_BLOCK - 1) <= t_pos
        c_valid_f = jnp.where(c_valid, 1.0, 0.0)
        ps = []
        for g in range(G):
            sg = jnp.where(c_valid, res["s"][:, g * tq:(g + 1) * tq], NEG)
            e = jnp.exp2(sg - jnp.max(sg, axis=0, keepdims=True))
            ps.append(e / jnp.sum(e, axis=0, keepdims=True) * c_valid_f)
        res["o_cmp"] = _dot(res["kvc"].T.astype(BF16), jnp.concatenate(ps, axis=1).astype(BF16))
        p_sum = ps[0]
        for g in range(1, G):
            p_sum = p_sum + ps[g]
        jb0 = lax.broadcasted_iota(jnp.int32, (LANES, LANES), 0) * SEL_BLOCK
        cb = lax.broadcasted_iota(jnp.int32, (LANES, LANES), 1) * CMP_STRIDE
        overlap = jnp.where((cb < jb0 + SEL_BLOCK) & (cb + CMP_BLOCK > jb0), 1.0, 0.0)
        overlap = overlap.astype(BF16)
        p_hi = p_sum.astype(BF16)
        p_lo = (p_sum - p_hi.astype(F32)).astype(BF16)
        res["imp"] = (_dot(overlap, p_hi) + _dot(overlap, p_lo))[:n_sel]

    def select():
        jrow = lax.broadcasted_iota(jnp.int32, (n_sel, tq), 0)
        tb = jnp.right_shift(qi * tq + lax.broadcasted_iota(jnp.int32, (n_sel, tq), 1),
                             SEL_SHIFT)
        forced = (jrow == 0) | (jrow == tb) | (jrow == tb - 1)
        score = jnp.where(forced, FORCE_SCORE, jnp.where(jrow <= tb, res["imp"], -1.0))
        rank = jnp.zeros((n_sel, tq), F32)
        for j2 in range(n_sel):
            other = score[j2:j2 + 1, :]
            beats = (other > score) | ((other == score) & (jrow > j2))
            rank = rank + jnp.where(beats, 1.0, 0.0)
        sel_bias = jnp.where(rank < min(SEL_TOPN, n_sel), 0.0, NEG)
        zrest = jnp.zeros((LANES - HEAD_DIM - n_sel, tq), F32)
        res["qst_sel"] = jnp.concatenate(
            [jnp.concatenate([xr, sel_bias, zrest], axis=0) for xr in rot], axis=1).astype(BF16)

    chain = [cmp_softmax, select]

    cmp_scores()
    fl = _Flash(qst_win, kw_scr, kwt_scr, *flash_scr)
    win_chunks = (tq + WIN - 1 + tkv - 1) // tkv
    first = diag0 + n_diag - win_chunks
    m = fl.m0
    fl.scores(first, 0)
    for c in range(win_chunks):
        if c + 1 < win_chunks:
            fl.scores(first + c + 1, (c + 1) % 2)
        if c < len(chain):
            chain[c]()
        m, a = fl.softmax(c % 2, m, band_bias(first + c))
        fl.values(first + c, c % 2, a)
    for stage in chain[win_chunks:]:
        stage()
    o_win = fl.result()
    o_cmp, qst_sel = res["o_cmp"], res["qst_sel"]

    fl = _Flash(qst_sel, ks_scr, kst_scr, *flash_scr)
    p_scr[1] = jnp.zeros((tkv, G * tq), BF16)
    fl.scores(0, 0)

    def pair(t, carry):
        m, a_prev = carry
        j0 = 2 * t
        fl.scores(j0 + 1, 1)
        fl.values(j0 - 1, 1, a_prev)
        m, a0 = fl.softmax(0, m)
        fl.scores(j0 + 2, 0)
        fl.values(j0, 0, a0)
        return fl.softmax(1, m)

    def run_pairs(n):
        def branch():
            carry = (fl.m0, fl.a0)
            for t in range(n):
                carry = pair(t, carry)
            return carry
        return branch

    max_pairs = (S // tkv + 1) // 2
    m, a_prev = lax.switch(diag0 // 2, [run_pairs(n) for n in range(max_pairs)])
    j0 = 2 * (diag0 // 2)
    fl.scores(j0 + 1, 1)
    fl.values(j0 - 1, 1, a_prev)
    m, a0 = fl.softmax(0, m, causal_bias(j0))
    fl.values(j0, 0, a0)
    m, a1 = fl.softmax(1, m, causal_bias(j0 + 1))
    fl.values(j0 + 1, 1, a1)
    o_sel = fl.result()

    heads = []
    for g in range(G):
        sl = slice(g * tq, (g + 1) * tq)
        gate = lambda i: gt_ref[0, pl.ds(i * NSA_HEADS + hk * G + g, 1), :]
        heads.append(gate(0) * o_cmp[HEAD_DIM:, sl] + gate(1) * o_sel[g] + gate(2) * o_win[g])
    o_ref[0] = jnp.concatenate(heads, axis=0).T.astype(o_ref.dtype)


def _nsa_attention(qt, cos_t, sin_t, tabs, kvcmp, kvs, kvw, gt, tq=NSA_Q_TILE,
                   tkv=NSA_KV_CHUNK):
    B, _, S = qt.shape
    HK = NSA_KV_HEADS
    gw = NSA_GROUP * HEAD_DIM
    tab_q = pl.BlockSpec((1, ROPE_HALF, tq), lambda b, h, i: (b, 0, i))
    tab_k = pl.BlockSpec((1, S, LANES), lambda b, h, i: (b, 0, 0))
    kv_spec = pl.BlockSpec((1, 1, S, LANES), lambda b, h, i: (b, h, 0, 0))
    kv_scr = pltpu.VMEM((S, LANES), BF16)
    kvt_scr = pltpu.VMEM((S // tkv, LANES, tkv), BF16)
    return pl.pallas_call(
        functools.partial(_nsa_attn_kernel, tq=tq, tkv=tkv),
        grid=(B, HK, S // tq),
        in_specs=[
            pl.BlockSpec((1, gw, tq), lambda b, h, i: (b, h, i)),
            tab_q, tab_q, tab_k, tab_k, tab_k,
            pl.BlockSpec((1, 1, kvcmp.shape[2], LANES), lambda b, h, i: (b, h, 0, 0)),
            kv_spec, kv_spec,
            pl.BlockSpec((1, LANES, tq), lambda b, h, i: (b, 0, i)),
        ],
        out_specs=pl.BlockSpec((1, tq, gw), lambda b, h, i: (b, i, h)),
        out_shape=jax.ShapeDtypeStruct((B, S, NSA_Q), BF16),
        scratch_shapes=[kv_scr, kvt_scr, kv_scr, kvt_scr,
                        pltpu.VMEM((2, tkv, NSA_GROUP * tq), F32),
                        pltpu.VMEM((2, tkv, NSA_GROUP * tq), BF16),
                        pltpu.VMEM((SUBLANES + HEAD_DIM, NSA_GROUP * tq), F32)],
        compiler_params=_cparams(("parallel", "parallel", "arbitrary")),
        name="nsa_attention",
    )(qt, cos_t, sin_t, *tabs, kvcmp, kvs, kvw, gt)


def _dil_combine(o_refs, l_refs, scratch):
    for ref, scr in zip((*o_refs, *l_refs), scratch):
        dil, rows = ref.shape[1], ref.shape[2]
        for r in range(dil):
            rows_r = slice(None) if dil == 1 else pl.ds(r, rows, stride=dil)
            for ct in range(ref.shape[3] // LANES):
                scr[ct, rows_r, :] = ref[0, r, :, ct * LANES:(ct + 1) * LANES].astype(F32)
    outs, lses = scratch[:3], [s[0] for s in scratch[3:]]
    mx = jnp.maximum(jnp.maximum(lses[0], lses[1]), lses[2])
    ex = [jnp.exp(l - mx) for l in lses]
    den = ex[0] + ex[1] + ex[2]
    alphas = [e / den for e in ex]
    D = o_refs[0].shape[3]
    col_head = jnp.right_shift(lax.broadcasted_iota(jnp.int32, (LANES, D), 1), HEAD_SHIFT)
    spread = jnp.where(col_head == lax.broadcasted_iota(jnp.int32, (LANES, D), 0),
                       1.0, 0.0).astype(BF16)
    wide = []
    for a in alphas:
        hi = a.astype(BF16)
        wide.append(_dot(hi, spread) + _dot((a - hi.astype(F32)).astype(BF16), spread))
    slabs = []
    for c in range(D // LANES):
        cols = slice(c * LANES, (c + 1) * LANES)
        acc = wide[0][:, cols] * outs[0][c]
        for n in range(1, 3):
            acc = acc + wide[n][:, cols] * outs[n][c]
        slabs.append(acc.astype(BF16))
    return jnp.concatenate(slabs, axis=1)


def _dil_mix_ffn_kernel(o0_ref, o1_ref, o2_ref, l0_ref, l1_ref, l2_ref, wo_ref, gmix_ref, h_ref,
                        gpre_ref, wgu_ref, wd_ref, gpost_ref, y_ref, *scratch):
    o = _dil_combine((o0_ref, o1_ref, o2_ref), (l0_ref, l1_ref, l2_ref), scratch)
    h = h_ref[0] + _rms(_dot(o, wo_ref[...]), gmix_ref[...])
    y_ref[0] = _ffn_block(h, gpre_ref, wgu_ref, wd_ref, gpost_ref)


def _dil_mix_ffn(outs, lses, w_o, g_mix, h, g_pre, w_gu, w_down, g_post, tm=ROW_TILE):
    B, S, D = h.shape
    row = pl.BlockSpec((1, tm, D), lambda b, i: (b, i, 0))
    vec = pl.BlockSpec((1, D), lambda b, i: (0, 0))
    once = pl.Buffered(1)
    cls = lambda x: pl.BlockSpec((1, x.shape[1], tm // x.shape[1], x.shape[3]),
                                 lambda b, i: (b, 0, i, 0))
    return pl.pallas_call(
        _dil_mix_ffn_kernel,
        grid=(B, S // tm),
        in_specs=[cls(x) for x in (*outs, *lses)] + [
            pl.BlockSpec((D, D), lambda b, i: (0, 0), pipeline_mode=once), vec, row, vec,
            pl.BlockSpec((D, 2 * D_FF), lambda b, i: (0, 0), pipeline_mode=once),
            pl.BlockSpec((D_FF, D), lambda b, i: (0, 0), pipeline_mode=once),
            vec],
        out_specs=row,
        out_shape=jax.ShapeDtypeStruct((B, S, D), F32),
        scratch_shapes=([pltpu.VMEM((D // LANES, tm, LANES), F32)] * 3
                        + [pltpu.VMEM((1, tm, LANES), F32)] * 3),
        compiler_params=_cparams(("parallel", "parallel")),
        name="dil_out_swiglu_ffn",
    )(*outs, *lses, w_o, g_mix, h, g_pre, w_gu, w_down, g_post)


def _class_major(ref, dil):
    rows = ref.shape[0] // dil
    return jnp.concatenate([ref[pl.ds(r, rows, stride=dil), :] for r in range(dil)], axis=0)


def _dil_proj_kernel(h_ref, c_ref, s1_ref, s2_ref, g_ref, w_ref, q_ref, k_ref, v_ref, x_scr,
                     *, dil):
    D = D_MODEL
    rows = h_ref.shape[1] // dil
    if dil == 1:
        x = h_ref[0]
        c, s1, s2 = c_ref[0], s1_ref[0], s2_ref[0]
    else:
        for ct in range(D // LANES):
            x_scr[ct] = h_ref[0, :, ct * LANES:(ct + 1) * LANES]
        x = jnp.concatenate([_class_major(x_scr.at[ct], dil) for ct in range(D // LANES)],
                            axis=1)
        c, s1, s2 = (_class_major(ref.at[0], dil) for ref in (c_ref, s1_ref, s2_ref))
    a = _rms(x, g_ref[...]).astype(BF16)
    w = w_ref[...]
    q = _rope_wide(_dot(a, w[:, :D]), c, s1, s2) * (SCALE * LOG2E)
    k = _rope_wide(_dot(a, w[:, D:2 * D]), c, s1, s2)
    v = _dot(a, w[:, 2 * D:])
    for r in range(dil):
        sl = slice(r * rows, (r + 1) * rows)
        q_ref[0, r] = q[sl].astype(BF16)
        k_ref[0, r] = k[sl].astype(BF16)
        v_ref[0, r] = v[sl].astype(BF16)


def _dil_proj(h, tabs, g_pre, w, dil, tm=ROW_TILE):
    B, S, D = h.shape
    out_shape = jax.ShapeDtypeStruct((B, dil, S // dil, D), BF16)
    out_spec = pl.BlockSpec((1, dil, tm // dil, D), lambda b, i: (b, 0, i, 0))
    tab_spec = pl.BlockSpec((1, tm, LANES), lambda b, i: (b, i, 0))
    return pl.pallas_call(
        functools.partial(_dil_proj_kernel, dil=dil),
        grid=(B, S // tm),
        in_specs=[
            pl.BlockSpec((1, tm, D), lambda b, i: (b, i, 0)),
            tab_spec, tab_spec, tab_spec,
            pl.BlockSpec((1, D), lambda b, i: (0, 0)),
            pl.BlockSpec((D, 3 * D), lambda b, i: (0, 0)),
        ],
        out_specs=[out_spec, out_spec, out_spec],
        out_shape=[out_shape, out_shape, out_shape],
        scratch_shapes=[pltpu.VMEM((D // LANES, tm, LANES), F32)],
        compiler_params=_cparams(("parallel", "parallel")),
        name=f"dil_proj_d{dil}",
    )(h, *tabs, g_pre, w)


def _dil_attn_kernel(q_ref, kp_ref, kc_ref, vp_ref, vc_ref, o_ref, lse_ref, *, band, has_prev):
    tq = DIL_BAND_BLOCK
    i = pl.program_id(2)
    nk = 2 * tq if has_prev else tq
    key = lax.broadcasted_iota(jnp.int32, (nk, tq), 0)
    rel = lax.broadcasted_iota(jnp.int32, (nk, tq), 1) + (nk - tq) - key
    mask = jnp.where(rel >= 0, jnp.where(rel <= band, 0.0, NEG), NEG)
    tr = lambda x: x.T
    zpad = jnp.zeros((HEAD_DIM, tq), BF16)

    subs = []
    if has_prev:
        qt = tr(q_ref[0, 0])
        k_all = jnp.concatenate([kp_ref[0, 0], kc_ref[0, 0]], axis=0)
        vt_all = jnp.concatenate([tr(vp_ref[0, 0]), tr(vc_ref[0, 0])], axis=1)
        before = jnp.where(i > 0, 0.0, NEG)
        first = jnp.where(key >= tq, mask, mask + before)
        for u in range(DIL_SUB):
            rows = slice(u * tq, (u + 1) * tq)
            subs.append((qt[:, rows], k_all[u * tq:u * tq + nk], vt_all[:, u * tq:u * tq + nk],
                         first if u == 0 else mask, (0, rows)))
    else:
        for u in range(DIL_SUB):
            subs.append((tr(q_ref[0, u]), kc_ref[0, u], tr(vc_ref[0, u]), mask,
                         (u, slice(None))))

    scores = []
    for qt_u, k_u, _, _, _ in subs:
        for h in range(DIL_HEADS):
            qh = qt_u[h * HEAD_DIM:(h + 1) * HEAD_DIM]
            pair = jnp.concatenate([qh, zpad] if h % 2 == 0 else [zpad, qh], axis=0)
            scores.append(_dot(k_u[:, (h // 2) * LANES:(h // 2 + 1) * LANES], pair))
    probs, maxes = [], []
    for n, s in enumerate(scores):
        s = s + subs[n // DIL_HEADS][3]
        m = jnp.max(s, axis=0, keepdims=True)
        probs.append(jnp.exp2(s - m).astype(BF16))
        maxes.append(m)
    ones = jnp.ones((SUBLANES, nk), BF16)
    for u, (_, _, vt_u, _, (lead, rows)) in enumerate(subs):
        outs, lse = [], []
        for h in range(DIL_HEADS):
            vt_h = jnp.concatenate([ones, vt_u[h * HEAD_DIM:(h + 1) * HEAD_DIM]], axis=0)
            r = _dot(vt_h, probs[u * DIL_HEADS + h])
            l = r[:1]
            outs.append(r[SUBLANES:] / l)
            lse.append(maxes[u * DIL_HEADS + h] * LN2 + jnp.log(l))
        o_ref[0, lead, rows, :] = jnp.concatenate(outs, axis=0).astype(o_ref.dtype).T
        lse.append(jnp.zeros((LANES - DIL_HEADS, tq), F32))
        lse_ref[0, lead, rows, :] = jnp.concatenate(lse, axis=0).T


def _dil_attention(q, k, v, band):
    B, dil, L, D = q.shape
    tq = DIL_BAND_BLOCK
    has_prev = L > tq
    if has_prev:
        grid = (B, dil, L // (DIL_SUB * tq))
        shape = lambda w: (1, 1, DIL_SUB * tq, w)
        cur = lambda w: pl.BlockSpec(shape(w), lambda b, r, i: (b, r, i, 0))
    else:
        grid = (B, dil // DIL_SUB, 1)
        shape = lambda w: (1, DIL_SUB, tq, w)
        cur = lambda w: pl.BlockSpec(shape(w), lambda b, r, i: (b, r, 0, 0))
    prev = pl.BlockSpec((1, 1, tq, D), lambda b, r, i: (b, r, jnp.maximum(DIL_SUB * i - 1, 0), 0))
    return pl.pallas_call(
        functools.partial(_dil_attn_kernel, band=band, has_prev=has_prev),
        grid=grid,
        in_specs=[cur(D), prev, cur(D), prev, cur(D)],
        out_specs=[cur(D), cur(LANES)],
        out_shape=[jax.ShapeDtypeStruct((B, dil, L, D), BF16),
                   jax.ShapeDtypeStruct((B, dil, L, LANES), F32)],
        compiler_params=_cparams(("parallel", "parallel", "arbitrary")),
        name=f"dil_attn_d{dil}",
    )(q, k, k, v, v)


def _ffn_block(h, gpre_ref, wgu_ref, wd_ref, gpost_ref):
    a = _rms(h, gpre_ref[...]).astype(BF16)
    acc = None
    for j in range(D_FF // FFN_CHUNK):
        cols = slice(j * FFN_CHUNK, (j + 1) * FFN_CHUNK)
        ucols = slice(D_FF + j * FFN_CHUNK, D_FF + (j + 1) * FFN_CHUNK)
        g = _dot(a, wgu_ref[:, cols])
        u = _dot(a, wgu_ref[:, ucols])
        mid = (g * _sigmoid(g) * u).astype(BF16)
        part = _dot(mid, wd_ref[cols, :])
        acc = part if acc is None else acc + part
    return h + _rms(acc, gpost_ref[...])


def _mix_ffn_kernel(o_ref, wo_ref, gmix_ref, h_ref, gpre_ref, wgu_ref, wd_ref, gpost_ref, y_ref):
    h = h_ref[0] + _rms(_dot(o_ref[0], wo_ref[...]), gmix_ref[...])
    y_ref[0] = _ffn_block(h, gpre_ref, wgu_ref, wd_ref, gpost_ref)


def _mix_ffn(o, w_o, g_mix, h, g_pre, w_gu, w_down, g_post, tm=ROW_TILE):
    B, S, D = h.shape
    row = pl.BlockSpec((1, tm, D), lambda b, i: (b, i, 0))
    vec = pl.BlockSpec((1, D), lambda b, i: (0, 0))
    once = pl.Buffered(1)
    return pl.pallas_call(
        _mix_ffn_kernel,
        grid=(B, S // tm),
        in_specs=[row, pl.BlockSpec((D, D), lambda b, i: (0, 0), pipeline_mode=once), vec,
                  row, vec,
                  pl.BlockSpec((D, 2 * D_FF), lambda b, i: (0, 0), pipeline_mode=once),
                  pl.BlockSpec((D_FF, D), lambda b, i: (0, 0), pipeline_mode=once),
                  vec],
        out_specs=row,
        out_shape=jax.ShapeDtypeStruct((B, S, D), F32),
        compiler_params=_cparams(("parallel", "parallel")),
        name="mixer_out_swiglu_ffn",
    )(o, w_o, g_mix, h, g_pre, w_gu, w_down, g_post)


def _nsa_mixer(h, rope, g_pre, w_in, cmp_pos, wk1, wk2, wv1, wv2):
    tabs, cos_t, sin_t = rope
    qt, kvc, kvs, kvw, gt = _nsa_proj(h, g_pre, *_nsa_in_weight(w_in))
    kvcmp = _compress(kvc, *_compress_weights(cmp_pos, wk1, wk2, wv1, wv2))
    return _nsa_attention(qt, cos_t, sin_t, tabs, kvcmp, kvs, kvw, gt)


def _dil_mixer(h, tabs, g_pre, w_in):
    D = D_MODEL
    outs, lses = [], []
    for gi, (window, dil) in enumerate(DIL_PATTERNS):
        w = w_in[:, gi * 3 * D:(gi + 1) * 3 * D].astype(BF16)
        q, k, v = _dil_proj(h, tabs, g_pre, w, dil)
        o, lse = _dil_attention(q, k, v, window // dil)
        outs.append(o)
        lses.append(lse)
    return outs, lses


def kernel(x, positions, norm_mix_pre, norm_mix_post, norm_ffn_pre, norm_ffn_post, ffn_w_gu,
           ffn_w_down, nsa_w_in, nsa_cmp_pos, nsa_cmp_wk1, nsa_cmp_wk2, nsa_cmp_wv1,
           nsa_cmp_wv2, nsa_w_o, dil_w_in, dil_w_o):
    depth = norm_mix_pre.shape[0]
    rope = _rope_tables(positions)
    tabs = rope[0]
    vec = lambda g: g.reshape(1, D_MODEL)
    h = x
    for i in range(depth):
        j = i // 2
        ffn = (vec(norm_ffn_pre[i]), ffn_w_gu[i].astype(BF16), ffn_w_down[i].astype(BF16),
               vec(norm_ffn_post[i]))
        if i % 2 == 0:
            o = _nsa_mixer(h, rope, vec(norm_mix_pre[i]), nsa_w_in[j], nsa_cmp_pos[j],
                           nsa_cmp_wk1[j], nsa_cmp_wk2[j], nsa_cmp_wv1[j], nsa_cmp_wv2[j])
            h = _mix_ffn(o, nsa_w_o[j].astype(BF16), vec(norm_mix_post[i]), h, *ffn)
        else:
            outs, lses = _dil_mixer(h, tabs, vec(norm_mix_pre[i]), dil_w_in[j])
            h = _dil_mix_ffn(outs, lses, dil_w_o[j].astype(BF16), vec(norm_mix_post[i]), h, *ffn)
    return h
```
